```python
import jax, jax.numpy as jnp
from jax import lax
import numpy as np

D_MODEL = 1024
BATCH = 8
SEQ = 4096
DEPTH = 4

N_META = 16
BLOCK = 128
N_PAD = (-N_META) % BLOCK

SB_HEADS = 8
SB_HEAD_DIM = 64
MLA_HEADS = 8
MLA_NOPE = 64
MLA_ROPE = 32
MLA_V = 64
MLA_Q_LORA = 384
MLA_KV_LORA = 256
RET_HEADS = 4
RET_QK = 64
RET_V = 128

D_SB = SB_HEADS * SB_HEAD_DIM
D_MLA = MLA_HEADS * MLA_V
D_RET = RET_HEADS * RET_V
D_MIX = D_SB + D_MLA + D_RET
D_FF = 4 * D_MODEL

IN_SIZES = (D_SB, D_SB, D_SB,
            MLA_Q_LORA, MLA_KV_LORA, MLA_ROPE,
            RET_HEADS * RET_QK, RET_HEADS * RET_QK, D_RET, D_RET)
IN_SPLITS = tuple(int(s) for s in np.cumsum(IN_SIZES)[:-1])
N_IN = int(sum(IN_SIZES))

ROPE_THETA = 10000.0
LN_EPS = 1e-5
DN_ALPHA = (2 * DEPTH) ** 0.25
DN_BETA = (8 * DEPTH) ** -0.25
RET_GAMMA = tuple(1.0 - 2.0 ** (-5 - h) for h in range(RET_HEADS))

kernel_name = "hybrid_sb_mla_retention_deepnorm"


def layer_norm(x, g, b):
    x32 = x.astype(jnp.float32)
    mu = jnp.mean(x32, -1, keepdims=True)
    var = jnp.mean(jnp.square(x32 - mu), -1, keepdims=True)
    y = (x32 - mu) * lax.rsqrt(var + LN_EPS)
    return (y * g.astype(jnp.float32) + b.astype(jnp.float32)).astype(x.dtype)


def rms_norm(x, g):
    x32 = x.astype(jnp.float32)
    y = x32 * lax.rsqrt(jnp.mean(jnp.square(x32), -1, keepdims=True) + LN_EPS)
    return (y * g.astype(jnp.float32)).astype(x.dtype)


def head_norm(y):
    y32 = y.astype(jnp.float32)
    mu = jnp.mean(y32, -1, keepdims=True)
    var = jnp.mean(jnp.square(y32 - mu), -1, keepdims=True)
    return (y32 - mu) * lax.rsqrt(var + LN_EPS)


def apply_rope(x, pos):
    half = x.shape[-1] // 2
    inv = ROPE_THETA ** (-jnp.arange(half, dtype=jnp.float32) / half)
    ang = pos[:, None] * inv[None, :]
    cos = jnp.cos(ang)[:, None, :]
    sin = jnp.sin(ang)[:, None, :]
    x1, x2 = x[..., :half], x[..., half:]
    return jnp.concatenate([x1 * cos - x2 * sin, x1 * sin + x2 * cos], -1).astype(x.dtype)


def stick_breaking_attention(q, k, v, valid):
    L, d = q.shape[1], q.shape[-1]
    scale = d ** -0.5
    outs = []
    for i in range(L // BLOCK):
        q0, q1 = i * BLOCK, (i + 1) * BLOCK
        z = jnp.einsum("bqhd,bkhd->bhqk", q[:, q0:q1], k[:, :q1]).astype(jnp.float32) * scale
        t_idx = jnp.arange(q0, q1)[:, None]
        s_idx = jnp.arange(q1)[None, :]
        mask = (s_idx < t_idx) & valid[None, :q1]
        log_beta = jnp.where(mask, jax.nn.log_sigmoid(z), -jnp.inf)
        log_keep = jnp.where(mask, jax.nn.log_sigmoid(-z), 0.0)
        incl = lax.cumsum(log_keep, axis=3, reverse=True)
        excl = jnp.concatenate([incl[..., 1:], jnp.zeros_like(incl[..., :1])], axis=-1)
        w = jnp.exp(log_beta + excl)
        outs.append(jnp.einsum("bhqk,bkhd->bqhd", w.astype(v.dtype), v[:, :q1]))
    return jnp.concatenate(outs, axis=1)


def mla_attention(q_nope, q_rope, k_nope, k_rope, v, valid):
    L = q_nope.shape[1]
    scale = (MLA_NOPE + MLA_ROPE) ** -0.5
    outs = []
    for i in range(L // BLOCK):
        q0, q1 = i * BLOCK, (i + 1) * BLOCK
        s = (jnp.einsum("bqhd,bkhd->bhqk", q_nope[:, q0:q1], k_nope[:, :q1])
             + jnp.einsum("bqhd,bkd->bhqk", q_rope[:, q0:q1], k_rope[:, :q1])).astype(jnp.float32) * scale
        t_idx = jnp.arange(q0, q1)[:, None]
        s_idx = jnp.arange(q1)[None, :]
        mask = (s_idx <= t_idx) & (valid[None, :q1] | (s_idx == t_idx))
        p = jax.nn.softmax(jnp.where(mask, s, -jnp.inf), axis=-1)
        outs.append(jnp.einsum("bhqk,bkhd->bqhd", p.astype(v.dtype), v[:, :q1]))
    return jnp.concatenate(outs, axis=1)


def multiscale_retention(q, k, v):
    B, L, H, dk = q.shape
    dv = v.shape[-1]
    n = L // BLOCK
    log_g = jnp.log(jnp.array(RET_GAMMA, jnp.float32))
    idx = jnp.arange(BLOCK, dtype=jnp.float32)
    diff = idx[:, None] - idx[None, :]
    d_in = jnp.where(diff[None] >= 0, jnp.exp(jnp.maximum(diff, 0.0)[None] * log_g[:, None, None]), 0.0)
    q_decay = jnp.exp((idx[:, None] + 1.0) * log_g[None, :])
    k_decay = jnp.exp((BLOCK - 1.0 - idx[:, None]) * log_g[None, :])
    c_decay = jnp.exp(BLOCK * log_g)

    def to_chunks(a):
        return jnp.moveaxis(a.astype(jnp.float32).reshape(B, n, BLOCK, H, a.shape[-1]), 1, 0)

    def step(state, inp):
        qc, kc, vc = inp
        inner = jnp.einsum("bqhd,bkhd->bhqk", qc, kc) * d_in[None]
        y = (jnp.einsum("bhqk,bkhe->bqhe", inner, vc)
             + jnp.einsum("bqhd,bhde->bqhe", qc, state) * q_decay[None, :, :, None])
        state = (state * c_decay[None, :, None, None]
                 + jnp.einsum("bkhd,bkhe->bhde", kc * k_decay[None, :, :, None], vc))
        return state, y

    state0 = jnp.zeros((B, H, dk, dv), jnp.float32)
    _, ys = lax.scan(step, state0, (to_chunks(q), to_chunks(k), to_chunks(v)))
    return jnp.moveaxis(ys, 0, 1).reshape(B, L, H, dv)


def hybrid_mixer(h, w_in, q_norm_g, kv_norm_g, w_uq, w_ukv, w_out, pos, valid):
    B, L, _ = h.shape
    proj = h @ w_in
    sb_q, sb_k, sb_v, c_q, c_kv, k_r, r_q, r_k, r_v, r_g = jnp.split(proj, IN_SPLITS, axis=-1)

    hs = lambda a, nh: a.reshape(B, L, nh, -1)
    out_a = stick_breaking_attention(hs(sb_q, SB_HEADS), hs(sb_k, SB_HEADS), hs(sb_v, SB_HEADS), valid)

    q = (rms_norm(c_q, q_norm_g) @ w_uq).reshape(B, L, MLA_HEADS, MLA_NOPE + MLA_ROPE)
    q_nope, q_rope = q[..., :MLA_NOPE], apply_rope(q[..., MLA_NOPE:], pos)
    kv = (rms_norm(c_kv, kv_norm_g) @ w_ukv).reshape(B, L, MLA_HEADS, MLA_NOPE + MLA_V)
    k_nope, v_b = kv[..., :MLA_NOPE], kv[..., MLA_NOPE:]
    k_rope = apply_rope(k_r[:, :, None, :], pos)[:, :, 0]
    out_b = mla_attention(q_nope, q_rope, k_nope, k_rope, v_b, valid)

    rq = apply_rope(hs(r_q, RET_HEADS), pos)
    rk = apply_rope(hs(r_k, RET_HEADS), pos) * (RET_QK ** -0.5)
    rk = jnp.where(valid[None, :, None, None], rk, jnp.zeros_like(rk))
    y_c = head_norm(multiscale_retention(rq, rk, hs(r_v, RET_HEADS))).reshape(B, L, D_RET)
    out_c = (jax.nn.silu(r_g.astype(jnp.float32)) * y_c).astype(h.dtype)

    mixed = jnp.concatenate([out_a.reshape(B, L, D_SB), out_b.reshape(B, L, D_MLA), out_c], axis=-1)
    return mixed @ w_out


def squared_relu_mlp(h, w1, w2):
    return jnp.square(jax.nn.relu(h @ w1)) @ w2


def setup_inputs(seed: int = 0) -> dict:
    key = jax.random.key(seed)
    ks = jax.random.split(key, 16)
    f32 = jnp.float32

    def nrm(k, shape, std):
        return jax.random.normal(k, shape, f32) * std

    return {
        "x": nrm(ks[0], (BATCH, SEQ, D_MODEL), 1.0),
        "meta_tokens": nrm(ks[1], (N_META, D_MODEL), 1.0),
        "ln_emb_g": 1.0 + nrm(ks[2], (D_MODEL,), 0.02),
        "ln_emb_b": nrm(ks[3], (D_MODEL,), 0.02),
        "w_in": nrm(ks[4], (DEPTH, D_MODEL, N_IN), D_MODEL ** -0.5),
        "mla_q_norm": 1.0 + nrm(ks[5], (DEPTH, MLA_Q_LORA), 0.02),
        "mla_kv_norm": 1.0 + nrm(ks[6], (DEPTH, MLA_KV_LORA), 0.02),
        "w_uq": nrm(ks[7], (DEPTH, MLA_Q_LORA, MLA_HEADS * (MLA_NOPE + MLA_ROPE)), MLA_Q_LORA ** -0.5),
        "w_ukv": nrm(ks[8], (DEPTH, MLA_KV_LORA, MLA_HEADS * (MLA_NOPE + MLA_V)), MLA_KV_LORA ** -0.5),
        "w_out": nrm(ks[9], (DEPTH, D_MIX, D_MODEL), DN_BETA * D_MIX ** -0.5),
        "ln1_g": 1.0 + nrm(ks[10], (DEPTH, D_MODEL), 0.02),
        "ln1_b": nrm(ks[11], (DEPTH, D_MODEL), 0.02),
        "w_ff1": nrm(ks[12], (DEPTH, D_MODEL, D_FF), D_MODEL ** -0.5),
        "w_ff2": nrm(ks[13], (DEPTH, D_FF, D_MODEL), DN_BETA * D_FF ** -0.5),
        "ln2_g": 1.0 + nrm(ks[14], (DEPTH, D_MODEL), 0.02),
        "ln2_b": nrm(ks[15], (DEPTH, D_MODEL), 0.02),
    }


def reference(x, meta_tokens, ln_emb_g, ln_emb_b, w_in, mla_q_norm, mla_kv_norm, w_uq, w_ukv, w_out,
              ln1_g, ln1_b, w_ff1, w_ff2, ln2_g, ln2_b):
    B, S, _ = x.shape
    meta = jnp.broadcast_to(meta_tokens[None].astype(x.dtype), (B, N_META, D_MODEL))
    pad = jnp.zeros((B, N_PAD, D_MODEL), x.dtype)
    h = jnp.concatenate([pad, meta, x], axis=1)
    L = h.shape[1]
    pos_i = jnp.arange(L) - N_PAD
    valid = pos_i >= 0
    pos = pos_i.astype(jnp.float32)
    h = layer_norm(h, ln_emb_g, ln_emb_b)
    for l in range(DEPTH):
        mix = hybrid_mixer(h, w_in[l], mla_q_norm[l], mla_kv_norm[l], w_uq[l], w_ukv[l], w_out[l], pos, valid)
        h = layer_norm(DN_ALPHA * h + mix, ln1_g[l], ln1_b[l])
        h = layer_norm(DN_ALPHA * h + squared_relu_mlp(h, w_ff1[l], w_ff2[l]), ln2_g[l], ln2_b[l])
    return h[:, N_PAD + N_META:]
```

```python
import functools

import numpy as np
import jax
import jax.numpy as jnp
from jax import lax
from jax.experimental import pallas as pl
from jax.experimental.pallas import tpu as pltpu

D_MODEL = 1024
DEPTH = 4
N_META = 16
BLOCK = 128
N_PAD = (-N_META) % BLOCK

SB_HEADS, SB_HEAD_DIM = 8, 64
MLA_HEADS, MLA_NOPE, MLA_ROPE, MLA_V = 8, 64, 32, 64
MLA_Q_LORA, MLA_KV_LORA = 384, 256
RET_HEADS, RET_QK, RET_V = 4, 64, 128

D_SB = SB_HEADS * SB_HEAD_DIM
D_MLA = MLA_HEADS * MLA_V
D_RET = RET_HEADS * RET_V
D_MIX = D_SB + D_MLA + D_RET
D_FF = 4 * D_MODEL
D_RQK = RET_HEADS * RET_QK

ROPE_THETA = 10000.0
LN_EPS = 1e-5
DN_ALPHA = (2 * DEPTH) ** 0.25
RET_GAMMA = tuple(1.0 - 2.0 ** (-5 - h) for h in range(RET_HEADS))

LANES = 128
MLA_QK_PAD = LANES

C_SBQ, C_SBK, C_SBV = 0, D_SB, 2 * D_SB
C_CQ = 3 * D_SB
C_CKV = C_CQ + MLA_Q_LORA
C_KR = C_CKV + MLA_KV_LORA
C_RQ = C_KR + LANES
C_RK = C_RQ + D_RQK
C_RV = C_RK + D_RQK
C_RG = C_RV + D_RET
N_IN_PAD = C_RG + D_RET

VMEM_LIMIT_BYTES = 56 * 1024 * 1024

SB_LOG_CUTOFF = 105.0
SB_GROUP = 3
NEG_BIG = -1e30


def _row_tile(length, cap):
    best = None
    for t in range(16, min(length, cap) + 1, 16):
        if length % t == 0:
            best = t
    assert best is not None, length
    return best


def _cparams(*sem):
    return pltpu.CompilerParams(dimension_semantics=sem, vmem_limit_bytes=VMEM_LIMIT_BYTES)


def _layer_norm_rows(x, g, b):
    mu = jnp.mean(x, axis=-1, keepdims=True)
    xc = x - mu
    var = jnp.mean(xc * xc, axis=-1, keepdims=True)
    return xc * lax.rsqrt(var + LN_EPS) * g + b


def _rms_norm_rows(x, g):
    return x * lax.rsqrt(jnp.mean(x * x, axis=-1, keepdims=True) + LN_EPS) * g


def _dot(a, b):
    return jnp.dot(a, b, preferred_element_type=jnp.float32)


def _dot_nt(a, b):
    return lax.dot_general(a, b, (((1,), (1,)), ((), ())), preferred_element_type=jnp.float32)


def _dot_tn(a, b):
    return lax.dot_general(a, b, (((0,), (0,)), ((), ())), preferred_element_type=jnp.float32)


def _ln_kernel(x_ref, g_ref, b_ref, o_ref):
    o_ref[...] = _layer_norm_rows(x_ref[...], g_ref[...], b_ref[...])


def _embed_layer_norm(h, g, b):
    bsz, length, d = h.shape
    tm = _row_tile(length, 1408)
    return pl.pallas_call(
        _ln_kernel,
        grid=(bsz, length // tm),
        in_specs=[pl.BlockSpec((1, tm, d), lambda i, j: (i, j, 0)),
                  pl.BlockSpec((1, 1, d), lambda i, j: (0, 0, 0)),
                  pl.BlockSpec((1, 1, d), lambda i, j: (0, 0, 0))],
        out_specs=pl.BlockSpec((1, tm, d), lambda i, j: (i, j, 0)),
        out_shape=jax.ShapeDtypeStruct(h.shape, jnp.float32),
        compiler_params=_cparams("parallel", "parallel"),
        name="embed_ln",
    )(h, g.reshape(1, 1, d), b.reshape(1, 1, d))


def _rope_tile(x, tab_ref, half):
    c, s1, s2 = tab_ref[0], tab_ref[1], tab_ref[2]
    return x * c + pltpu.roll(x, LANES - half, 1) * s1 + pltpu.roll(x, half, 1) * s2


def _proj_kernel(h_ref, win_ref, qn_ref, kvn_ref, wuq_ref, wuk_ref, wuv_ref, mtab_ref, rtab_ref,
                 sbq_ref, sbk_ref, sbv_ref, mq_ref, mk_ref, mv_ref, rq_ref, rk_ref, rv_ref, rg_ref,
                 *, tm):
    bf16 = jnp.bfloat16
    hb = h_ref[0].astype(bf16)

    def seg(lo, hi):
        return _dot(hb, win_ref[:, lo:hi])

    sbq_ref[0] = (seg(C_SBQ, C_SBK) * (SB_HEAD_DIM ** -0.5)).astype(bf16)
    sbk_ref[0] = seg(C_SBK, C_SBV).astype(bf16)
    sbv_ref[0] = seg(C_SBV, C_CQ).astype(bf16)

    cq = _rms_norm_rows(seg(C_CQ, C_CKV), qn_ref[...])
    q = _dot(cq.astype(bf16), wuq_ref[...])
    mla_scale = (MLA_NOPE + MLA_ROPE) ** -0.5
    for hd in range(MLA_HEADS):
        lo = hd * MLA_QK_PAD
        blk = _rope_tile(q[:, lo:lo + MLA_QK_PAD], mtab_ref, MLA_ROPE // 2)
        mq_ref[0, :, lo:lo + MLA_QK_PAD] = (blk * mla_scale).astype(bf16)
    ckv = _rms_norm_rows(seg(C_CKV, C_KR), kvn_ref[...]).astype(bf16)
    kn = _dot(ckv, wuk_ref[...])
    kr = _rope_tile(seg(C_KR, C_RQ), mtab_ref, MLA_ROPE // 2)
    for hd in range(MLA_HEADS):
        lo = hd * MLA_QK_PAD
        mk_ref[0, :, lo:lo + MLA_QK_PAD] = (kn[:, lo:lo + MLA_QK_PAD] + kr).astype(bf16)
    mv_ref[0] = _dot(ckv, wuv_ref[...]).astype(bf16)

    row = lax.broadcasted_iota(jnp.int32, (tm, LANES), 0) + pl.program_id(1) * tm
    kscale = jnp.where(row >= N_PAD, RET_QK ** -0.5, 0.0).astype(jnp.float32)
    rq = seg(C_RQ, C_RK)
    rk = seg(C_RK, C_RV)
    for grp in range(D_RQK // LANES):
        lo = grp * LANES
        rq_ref[0, :, lo:lo + LANES] = _rope_tile(rq[:, lo:lo + LANES], rtab_ref, RET_QK // 2).astype(bf16)
        rk_ref[0, :, lo:lo + LANES] = (_rope_tile(rk[:, lo:lo + LANES], rtab_ref, RET_QK // 2)
                                       * kscale).astype(bf16)
    rv_ref[0] = seg(C_RV, C_RG).astype(bf16)
    rg_ref[0] = seg(C_RG, N_IN_PAD).astype(bf16)


def _projections(h, win, qn, kvn, wuq, wuk, wuv, mtab, rtab):
    bsz, length, d = h.shape
    tm = _row_tile(length, 704)
    const2 = lambda i, j: (0, 0)
    tok = lambda w: pl.BlockSpec((1, tm, w), lambda i, j: (i, j, 0))
    widths = (D_SB, D_SB, D_SB, MLA_HEADS * MLA_QK_PAD, MLA_HEADS * MLA_QK_PAD, D_MLA,
              D_RQK, D_RQK, D_RET, D_RET)
    return pl.pallas_call(
        functools.partial(_proj_kernel, tm=tm),
        grid=(bsz, length // tm),
        in_specs=[tok(d),
                  pl.BlockSpec(win.shape, const2),
                  pl.BlockSpec(qn.shape, const2),
                  pl.BlockSpec(kvn.shape, const2),
                  pl.BlockSpec(wuq.shape, const2),
                  pl.BlockSpec(wuk.shape, const2),
                  pl.BlockSpec(wuv.shape, const2),
                  pl.BlockSpec((3, tm, LANES), lambda i, j: (0, j, 0)),
                  pl.BlockSpec((3, tm, LANES), lambda i, j: (0, j, 0))],
        out_specs=[tok(w) for w in widths],
        out_shape=[jax.ShapeDtypeStruct((bsz, length, w), jnp.bfloat16) for w in widths],
        compiler_params=_cparams("parallel", "parallel"),
        name="projections",
    )(h, win, qn, kvn, wuq, wuk, wuv, mtab, rtab)


def _sb_block(qm, kb, vb, mask, r_prev, uo):
    bf16 = jnp.bfloat16
    z = _dot_nt(qm, kb)
    sp = jnp.log(1.0 + jnp.exp(-jnp.abs(z)))
    lb = jnp.minimum(z, 0.0) - sp
    lk = jnp.where(mask, lb - z, 0.0)
    hi = lk.astype(bf16)
    mid = (lk - hi.astype(jnp.float32)).astype(bf16)
    ex = _dot(jnp.concatenate([hi, mid], axis=1), uo)
    w = jnp.where(mask, jnp.exp(lb + ex[:, :LANES] + r_prev), 0.0)
    return _dot(w.astype(bf16), vb), r_prev + ex[:, LANES:]


def _sb_kernel(q_ref, k_ref, v_ref, uo_ref, o_ref, r_ref, *, nblk):
    uo = uo_ref[...]
    col = lax.broadcasted_iota(jnp.int32, (BLOCK, LANES), 1)
    row = lax.broadcasted_iota(jnp.int32, (BLOCK, LANES), 0)
    head0 = col < SB_HEAD_DIM
    zero = jnp.zeros((BLOCK, LANES), jnp.bfloat16)

    def group(gi, carry):
        blocks = [gi * SB_GROUP + g for g in range(SB_GROUP)]
        rows = [pl.ds(pl.multiple_of(i * BLOCK, BLOCK), BLOCK) for i in blocks]
        qms = []
        for g in range(SB_GROUP):
            qb = q_ref[0, rows[g], :]
            qms.append((jnp.where(head0, qb, zero), jnp.where(head0, zero, qb)))

        for g in range(SB_GROUP):
            kb = k_ref[0, rows[g], :]
            vb = v_ref[0, rows[g], :]
            first_valid = jnp.where(blocks[g] == 0, N_PAD, 0)
            mask = (col < row) & (col >= first_valid)
            pv0, r0 = _sb_block(qms[g][0], kb, vb, mask, 0.0, uo)
            pv1, r1 = _sb_block(qms[g][1], kb, vb, mask, 0.0, uo)
            o_ref[0, rows[g], :] = jnp.where(head0, pv0, pv1)
            r_ref[g, 0] = r0
            r_ref[g, 1] = r1

        def body(c):
            d, _ = c
            need = jnp.float32(-jnp.inf)
            for g in range(SB_GROUP):
                j = blocks[g] - d
                jc = jnp.maximum(j, 0)
                krows = pl.ds(pl.multiple_of(jc * BLOCK, BLOCK), BLOCK)
                kb = k_ref[0, krows, :]
                vb = v_ref[0, krows, :]
                first_valid = jnp.where(j < 0, BLOCK, jnp.where(j == 0, N_PAD, 0))
                mask = col >= first_valid
                pv0, r0 = _sb_block(qms[g][0], kb, vb, mask, r_ref[g, 0], uo)
                pv1, r1 = _sb_block(qms[g][1], kb, vb, mask, r_ref[g, 1], uo)
                o_ref[0, rows[g], :] += jnp.where(head0, pv0, pv1)
                r_ref[g, 0] = r0
                r_ref[g, 1] = r1
                rmax = jnp.maximum(jnp.max(r0), jnp.max(r1))
                need = jnp.maximum(need, jnp.where(j >= 1, rmax, -jnp.inf))
            return d + 1, need > -SB_LOG_CUTOFF

        lax.while_loop(lambda c: c[1], body, (jnp.int32(1), blocks[-1] >= 1))
        return carry

    lax.fori_loop(0, nblk // SB_GROUP, group, 0)


def _stick_breaking(q, k, v, uo):
    bsz, length, _ = q.shape
    nblk = length // BLOCK
    assert nblk % SB_GROUP == 0
    pair = lambda: pl.BlockSpec((1, length, LANES), lambda b, p: (b, 0, p))
    return pl.pallas_call(
        functools.partial(_sb_kernel, nblk=nblk),
        grid=(bsz, D_SB // LANES),
        in_specs=[pair(), pair(), pair(), pl.BlockSpec(uo.shape, lambda b, p: (0, 0))],
        out_specs=pl.BlockSpec((1, length, LANES), lambda b, p: (b, 0, p)),
        out_shape=jax.ShapeDtypeStruct((bsz, length, D_SB), jnp.float32),
        scratch_shapes=[pltpu.VMEM((SB_GROUP, 2, BLOCK, LANES), jnp.float32)],
        compiler_params=_cparams("parallel", "parallel"),
        name="stick_breaking",
    )(q, k, v, uo)


def _mla_kernel(q_ref, k_ref, v_ref, o_ref, *, tq):
    qi = pl.program_id(2)
    bf16 = jnp.bfloat16
    row = lax.broadcasted_iota(jnp.int32, (tq, tq), 0)
    col = lax.broadcasted_iota(jnp.int32, (tq, tq), 1)
    lane = lax.broadcasted_iota(jnp.int32, (tq, LANES), 1)
    qs = [q_ref[0, :, hd * MLA_QK_PAD:(hd + 1) * MLA_QK_PAD] for hd in range(2)]

    def tile(j, carry, masked):
        krows = pl.ds(pl.multiple_of(j * tq, tq), tq)
        vb = v_ref[0, krows, :]
        if masked:
            t = row + qi * tq
            s_idx = col + j * tq
            mask = (s_idx <= t) & ((s_idx >= N_PAD) | (s_idx == t))
        out = []
        for hd in range(2):
            m, l, acc = carry[hd]
            s = _dot_nt(qs[hd], k_ref[0, krows, hd * MLA_QK_PAD:(hd + 1) * MLA_QK_PAD])
            if masked:
                s = jnp.where(mask, s, NEG_BIG)
            m_new = jnp.maximum(m, jnp.max(s, axis=-1, keepdims=True))
            alpha = jnp.exp(m - m_new)
            p = jnp.exp(s - m_new)
            l = alpha * l + jnp.sum(p, axis=-1, keepdims=True)
            acc = alpha * acc + _dot(p.astype(bf16), vb)
            out.append((m_new, l, acc))
        return tuple(out)

    init = tuple((jnp.full((tq, 1), NEG_BIG, jnp.float32), jnp.zeros((tq, 1), jnp.float32),
                  jnp.zeros((tq, LANES), jnp.float32)) for _ in range(2))
    carry = tile(0, init, True)
    carry = lax.fori_loop(1, qi, lambda j, c: tile(j, c, False), carry)
    carry = lax.cond(qi > 0, lambda c: tile(qi, c, True), lambda c: c, carry)
    o0 = carry[0][2] / carry[0][1]
    o1 = carry[1][2] / carry[1][1]
    o_ref[0] = jnp.where(lane < MLA_V, o0, o1).astype(o_ref.dtype)


def _mla_attention(q, k, v):
    bsz, length, _ = v.shape
    tq = _row_tile(length, 384)
    return pl.pallas_call(
        functools.partial(_mla_kernel, tq=tq),
        grid=(bsz, MLA_HEADS // 2, length // tq),
        in_specs=[pl.BlockSpec((1, tq, 2 * MLA_QK_PAD), lambda b, p, i: (b, i, p)),
                  pl.BlockSpec((1, length, 2 * MLA_QK_PAD), lambda b, p, i: (b, 0, p)),
                  pl.BlockSpec((1, length, LANES), lambda b, p, i: (b, 0, p))],
        out_specs=pl.BlockSpec((1, tq, LANES), lambda b, p, i: (b, i, p)),
        out_shape=jax.ShapeDtypeStruct((bsz, length, D_MLA), jnp.bfloat16),
        compiler_params=_cparams("parallel", "parallel", "arbitrary"),
        name="mla_attention",
    )(q, k, v)


def _ret_kernel(q_ref, k_ref, v_ref, g_ref, dec_ref, o_ref, *, nblk):
    bf16 = jnp.bfloat16
    lane = lax.broadcasted_iota(jnp.int32, (BLOCK, LANES), 1)
    head_mask = (lane < RET_QK, lane >= RET_QK)
    zero = jnp.zeros((BLOCK, LANES), bf16)

    def chunk(c, states):
        rows = pl.ds(pl.multiple_of(c * BLOCK, BLOCK), BLOCK)
        qb = q_ref[0, rows, :]
        kb = k_ref[0, rows, :]
        new_states = []
        for hd in range(2):
            cols = slice(hd * RET_V, (hd + 1) * RET_V)
            d_in, q_decay, k_decay, c_decay = (dec_ref[hd, t] for t in range(4))
            qc = jnp.where(head_mask[hd], qb, zero)
            kc = jnp.where(head_mask[hd], kb, zero)
            vc = v_ref[0, rows, cols]
            state = states[hd]
            inner = _dot_nt(qc, kc) * d_in
            y = _dot(inner.astype(bf16), vc) + _dot(qc, state.astype(bf16)) * q_decay
            kd = (kc.astype(jnp.float32) * k_decay).astype(bf16)
            new_states.append(state * c_decay + _dot_tn(kd, vc))
            mu = jnp.mean(y, axis=-1, keepdims=True)
            yc = y - mu
            var = jnp.mean(yc * yc, axis=-1, keepdims=True)
            yn = yc * lax.rsqrt(var + LN_EPS)
            gate = g_ref[0, rows, cols].astype(jnp.float32)
            o_ref[0, rows, cols] = (gate * jax.nn.sigmoid(gate) * yn).astype(o_ref.dtype)
        return tuple(new_states)

    zeros = jnp.zeros((BLOCK, LANES), jnp.float32)
    lax.fori_loop(0, nblk, chunk, (zeros, zeros))


def _retention(q, k, v, g, dec):
    bsz, length, _ = v.shape
    qk = lambda: pl.BlockSpec((1, length, LANES), lambda b, p: (b, 0, p))
    vg = lambda: pl.BlockSpec((1, length, 2 * RET_V), lambda b, p: (b, 0, p))
    return pl.pallas_call(
        functools.partial(_ret_kernel, nblk=length // BLOCK),
        grid=(bsz, RET_HEADS // 2),
        in_specs=[qk(), qk(), vg(), vg(),
                  pl.BlockSpec((2, 4, BLOCK, LANES), lambda b, p: (p, 0, 0, 0))],
        out_specs=pl.BlockSpec((1, length, 2 * RET_V), lambda b, p: (b, 0, p)),
        out_shape=jax.ShapeDtypeStruct((bsz, length, D_RET), jnp.bfloat16),
        compiler_params=_cparams("parallel", "parallel"),
        name="retention",
    )(q, k, v, g, dec)


def _out_kernel(a_ref, b_ref, c_ref, h_ref, w_ref, g_ref, beta_ref, o_ref):
    bf16 = jnp.bfloat16
    mix = (_dot(a_ref[0].astype(bf16), w_ref[0:D_SB, :])
           + _dot(b_ref[0], w_ref[D_SB:D_SB + D_MLA, :])
           + _dot(c_ref[0], w_ref[D_SB + D_MLA:D_MIX, :]))
    o_ref[0] = _layer_norm_rows(DN_ALPHA * h_ref[0] + mix, g_ref[...], beta_ref[...])


def _out_projection(a, b, c, h, w, g, beta):
    bsz, length, d = h.shape
    tm = _row_tile(length, 704)
    tok = lambda wd: pl.BlockSpec((1, tm, wd), lambda i, j: (i, j, 0))
    const2 = lambda i, j: (0, 0)
    return pl.pallas_call(
        _out_kernel,
        grid=(bsz, length // tm),
        in_specs=[tok(D_SB), tok(D_MLA), tok(D_RET), tok(d),
                  pl.BlockSpec(w.shape, const2), pl.BlockSpec(g.shape, const2),
                  pl.BlockSpec(beta.shape, const2)],
        out_specs=tok(d),
        out_shape=jax.ShapeDtypeStruct(h.shape, jnp.float32),
        compiler_params=_cparams("parallel", "parallel"),
        name="out_projection",
    )(a, b, c, h, w, g, beta)


FF_CHUNK = 1024


def _mlp_kernel(h_ref, w1_ref, w2_ref, g_ref, beta_ref, o_ref):
    bf16 = jnp.bfloat16
    h = h_ref[0]
    hb = h.astype(bf16)
    acc = DN_ALPHA * h
    for lo in range(0, D_FF, FF_CHUNK):
        u = jnp.maximum(_dot(hb, w1_ref[:, lo:lo + FF_CHUNK]), 0.0)
        acc = acc + _dot((u * u).astype(bf16), w2_ref[lo:lo + FF_CHUNK, :])
    o_ref[0] = _layer_norm_rows(acc, g_ref[...], beta_ref[...])


def _mlp(h, w1, w2, g, beta):
    bsz, length, d = h.shape
    tm = _row_tile(length, 704)
    tok = pl.BlockSpec((1, tm, d), lambda i, j: (i, j, 0))
    const2 = lambda i, j: (0, 0)
    return pl.pallas_call(
        _mlp_kernel,
        grid=(bsz, length // tm),
        in_specs=[tok, pl.BlockSpec(w1.shape, const2), pl.BlockSpec(w2.shape, const2),
                  pl.BlockSpec(g.shape, const2), pl.BlockSpec(beta.shape, const2)],
        out_specs=tok,
        out_shape=jax.ShapeDtypeStruct(h.shape, jnp.float32),
        compiler_params=_cparams("parallel", "parallel"),
        name="mlp",
    )(h, w1, w2, g, beta)


def _rope_tables(length):
    pos = (jnp.arange(length) - N_PAD).astype(jnp.float32)

    def angles(half):
        inv = ROPE_THETA ** (-jnp.arange(half, dtype=jnp.float32) / half)
        ang = pos[:, None] * inv[None, :]
        return jnp.cos(ang), jnp.sin(ang)

    cos, sin = angles(MLA_ROPE // 2)
    one = jnp.ones((length, MLA_NOPE), jnp.float32)
    z = lambda w: jnp.zeros((length, w), jnp.float32)
    mla = jnp.stack([jnp.concatenate([one, cos, cos, z(32)], 1),
                     jnp.concatenate([z(64), -sin, z(16), z(32)], 1),
                     jnp.concatenate([z(64), z(16), sin, z(32)], 1)])
    cos, sin = angles(RET_QK // 2)
    ret = jnp.stack([jnp.concatenate([cos, cos] * 2, 1),
                     jnp.concatenate([-sin, z(32)] * 2, 1),
                     jnp.concatenate([z(32), sin] * 2, 1)])
    return mla, ret


def _retention_decays():
    log_g = jnp.log(jnp.array(RET_GAMMA, jnp.float32))
    idx = jnp.arange(BLOCK, dtype=jnp.float32)
    diff = idx[:, None] - idx[None, :]
    d_in = jnp.where(diff[None] >= 0, jnp.exp(jnp.maximum(diff, 0.0)[None] * log_g[:, None, None]), 0.0)
    q_decay = jnp.exp((idx[None, :] + 1.0) * log_g[:, None])
    k_decay = jnp.exp((BLOCK - 1.0 - idx[None, :]) * log_g[:, None])
    c_decay = jnp.exp(BLOCK * log_g)
    rep = lambda a: jnp.broadcast_to(a[:, :, None], (RET_HEADS, BLOCK, LANES))
    full = jnp.broadcast_to(c_decay[:, None, None], (RET_HEADS, BLOCK, LANES))
    return jnp.stack([d_in, rep(q_decay), rep(k_decay), full], axis=1)


def _suffix_sum_matrix():
    i = np.arange(BLOCK)
    upper = (i[:, None] > i[None, :]).astype(np.float32)
    ones = np.ones((BLOCK, BLOCK), np.float32)
    half = np.concatenate([upper, ones], axis=1)
    return jnp.asarray(np.concatenate([half, half], axis=0), jnp.bfloat16)


def kernel(x, meta_tokens, ln_emb_g, ln_emb_b, w_in, mla_q_norm, mla_kv_norm, w_uq, w_ukv, w_out,
           ln1_g, ln1_b, w_ff1, w_ff2, ln2_g, ln2_b):
    bsz, seq, d = x.shape
    assert d == D_MODEL
    bf16 = jnp.bfloat16
    depth = w_in.shape[0]

    meta = jnp.broadcast_to(meta_tokens[None].astype(x.dtype), (bsz, N_META, d))
    h = jnp.concatenate([jnp.zeros((bsz, N_PAD, d), x.dtype), meta, x], axis=1)
    length = h.shape[1]
    assert length % BLOCK == 0

    zc = lambda n: jnp.zeros((depth, d, n), w_in.dtype)
    k_r_lo = 3 * D_SB + MLA_Q_LORA + MLA_KV_LORA
    win = jnp.concatenate([w_in[..., :k_r_lo], zc(MLA_NOPE), w_in[..., k_r_lo:k_r_lo + MLA_ROPE],
                           zc(LANES - MLA_NOPE - MLA_ROPE), w_in[..., k_r_lo + MLA_ROPE:]], axis=-1).astype(bf16)
    assert win.shape[-1] == N_IN_PAD
    wuq = w_uq.reshape(depth, MLA_Q_LORA, MLA_HEADS, MLA_NOPE + MLA_ROPE)
    wuq = jnp.pad(wuq, ((0, 0), (0, 0), (0, 0), (0, MLA_QK_PAD - MLA_NOPE - MLA_ROPE)))
    wuq = wuq.reshape(depth, MLA_Q_LORA, MLA_HEADS * MLA_QK_PAD).astype(bf16)
    wukv = w_ukv.reshape(depth, MLA_KV_LORA, MLA_HEADS, MLA_NOPE + MLA_V)
    wuk = jnp.pad(wukv[..., :MLA_NOPE], ((0, 0), (0, 0), (0, 0), (0, MLA_QK_PAD - MLA_NOPE)))
    wuk = wuk.reshape(depth, MLA_KV_LORA, MLA_HEADS * MLA_QK_PAD).astype(bf16)
    wuv = wukv[..., MLA_NOPE:].reshape(depth, MLA_KV_LORA, D_MLA).astype(bf16)
    wout = w_out.astype(bf16)
    w1 = w_ff1.astype(bf16)
    w2 = w_ff2.astype(bf16)

    mtab, rtab = _rope_tables(length)
    dec = _retention_decays()
    uo = _suffix_sum_matrix()
    row2 = lambda a: a.reshape(1, -1)

    h = _embed_layer_norm(h, ln_emb_g, ln_emb_b)
    for l in range(depth):
        sbq, sbk, sbv, mq, mk, mv, rq, rk, rv, rg = _projections(
            h, win[l], row2(mla_q_norm[l]), row2(mla_kv_norm[l]), wuq[l], wuk[l], wuv[l], mtab, rtab)
        out_a = _stick_breaking(sbq, sbk, sbv, uo)
        out_b = _mla_attention(mq, mk, mv)
        out_c = _retention(rq, rk, rv, rg, dec)
        h = _out_projection(out_a, out_b, out_c, h, wout[l], row2(ln1_g[l]), row2(ln1_b[l]))
        h = _mlp(h, w1[l], w2[l], row2(ln2_g[l]), row2(ln2_b[l]))
    return h[:, N_PAD + N_META:]
```

```python
import functools

import numpy as np
import jax
import jax.numpy as jnp
from jax import lax
from jax.experimental import pallas as pl
from jax.experimental.pallas import tpu as pltpu

D_MODEL = 1024
DEPTH = 4
N_META = 16
BLOCK = 128
N_PAD = (-N_META) % BLOCK

SB_HEADS, SB_HEAD_DIM = 8, 64
MLA_HEADS, MLA_NOPE, MLA_ROPE, MLA_V = 8, 64, 32, 64
MLA_Q_LORA, MLA_KV_LORA = 384, 256
RET_HEADS, RET_QK, RET_V = 4, 64, 128

D_SB = SB_HEADS * SB_HEAD_DIM
D_MLA = MLA_HEADS * MLA_V
D_RET = RET_HEADS * RET_V
D_MIX = D_SB + D_MLA + D_RET
D_FF = 4 * D_MODEL
D_RQK = RET_HEADS * RET_QK

ROPE_THETA = 10000.0
LN_EPS = 1e-5
DN_ALPHA = (2 * DEPTH) ** 0.25
RET_GAMMA = tuple(1.0 - 2.0 ** (-5 - h) for h in range(RET_HEADS))

LANES = 128
MLA_QK_PAD = LANES

C_SBQ, C_SBK, C_SBV = 0, D_SB, 2 * D_SB
C_CQ = 3 * D_SB
C_CKV = C_CQ + MLA_Q_LORA
C_KR = C_CKV + MLA_KV_LORA
C_RQ = C_KR + LANES
C_RK = C_RQ + D_RQK
C_RV = C_RK + D_RQK
C_RG = C_RV + D_RET
N_IN_PAD = C_RG + D_RET

VMEM_LIMIT_BYTES = 56 * 1024 * 1024

LOG2E = 1.4426950408889634
SB_LOG_CUTOFF = 105.0
SB_LOG2_CUTOFF = SB_LOG_CUTOFF * LOG2E
SB_GROUP_MAX = 11
NEG_BIG = -1e30


def _row_tile(length, cap):
    best = None
    for t in range(16, min(length, cap) + 1, 16):
        if length % t == 0:
            best = t
    assert best is not None, length
    return best


def _cparams(*sem):
    return pltpu.CompilerParams(dimension_semantics=sem, vmem_limit_bytes=VMEM_LIMIT_BYTES)


def _layer_norm_rows(x, g, b):
    mu = jnp.mean(x, axis=-1, keepdims=True)
    xc = x - mu
    var = jnp.mean(xc * xc, axis=-1, keepdims=True)
    return xc * lax.rsqrt(var + LN_EPS) * g + b


def _rms_norm_rows(x, g):
    return x * lax.rsqrt(jnp.mean(x * x, axis=-1, keepdims=True) + LN_EPS) * g


def _dot(a, b):
    return jnp.dot(a, b, preferred_element_type=jnp.float32)


def _dot_nt(a, b):
    return lax.dot_general(a, b, (((1,), (1,)), ((), ())), preferred_element_type=jnp.float32)


def _dot_tn(a, b):
    return lax.dot_general(a, b, (((0,), (0,)), ((), ())), preferred_element_type=jnp.float32)


def _ln_kernel(x_ref, g_ref, b_ref, o_ref):
    o_ref[...] = _layer_norm_rows(x_ref[...], g_ref[...], b_ref[...])


def _embed_layer_norm(h, g, b):
    bsz, length, d = h.shape
    tm = _row_tile(length, 1408)
    return pl.pallas_call(
        _ln_kernel,
        grid=(bsz, length // tm),
        in_specs=[pl.BlockSpec((1, tm, d), lambda i, j: (i, j, 0)),
                  pl.BlockSpec((1, 1, d), lambda i, j: (0, 0, 0)),
                  pl.BlockSpec((1, 1, d), lambda i, j: (0, 0, 0))],
        out_specs=pl.BlockSpec((1, tm, d), lambda i, j: (i, j, 0)),
        out_shape=jax.ShapeDtypeStruct(h.shape, jnp.float32),
        compiler_params=_cparams("parallel", "parallel"),
        name="embed_ln",
    )(h, g.reshape(1, 1, d), b.reshape(1, 1, d))


def _rope_tile(x, tab_ref, half):
    c, s1, s2 = tab_ref[0], tab_ref[1], tab_ref[2]
    return x * c + pltpu.roll(x, LANES - half, 1) * s1 + pltpu.roll(x, half, 1) * s2


def _proj_kernel(h_ref, win_ref, qn_ref, kvn_ref, wuq_ref, wuk_ref, wuv_ref, mtab_ref, rtab_ref,
                 sbq_ref, sbk_ref, sbv_ref, mq_ref, mk_ref, mv_ref, rq_ref, rk_ref, rv_ref, rg_ref,
                 *, tm):
    bf16 = jnp.bfloat16
    hb = h_ref[0].astype(bf16)

    def seg(lo, hi):
        return _dot(hb, win_ref[:, lo:hi])

    sbq_ref[0] = (seg(C_SBQ, C_SBK) * (LOG2E * SB_HEAD_DIM ** -0.5)).astype(bf16)
    sbk_ref[0] = seg(C_SBK, C_SBV).astype(bf16)
    sbv_ref[0] = seg(C_SBV, C_CQ).astype(bf16)

    cq = _rms_norm_rows(seg(C_CQ, C_CKV), qn_ref[...])
    q = _dot(cq.astype(bf16), wuq_ref[...])
    mla_scale = LOG2E * (MLA_NOPE + MLA_ROPE) ** -0.5
    for hd in range(MLA_HEADS):
        lo = hd * MLA_QK_PAD
        blk = _rope_tile(q[:, lo:lo + MLA_QK_PAD], mtab_ref, MLA_ROPE // 2)
        mq_ref[0, :, lo:lo + MLA_QK_PAD] = (blk * mla_scale).astype(bf16)
    ckv = _rms_norm_rows(seg(C_CKV, C_KR), kvn_ref[...]).astype(bf16)
    kn = _dot(ckv, wuk_ref[...])
    kr = _rope_tile(seg(C_KR, C_RQ), mtab_ref, MLA_ROPE // 2)
    for hd in range(MLA_HEADS):
        lo = hd * MLA_QK_PAD
        mk_ref[0, :, lo:lo + MLA_QK_PAD] = (kn[:, lo:lo + MLA_QK_PAD] + kr).astype(bf16)
    mv = _dot(ckv, wuv_ref[...])
    lane = lax.broadcasted_iota(jnp.int32, (tm, LANES), 1)
    for hd in range(MLA_HEADS):
        pair_tile = mv[:, (hd // 2) * LANES:(hd // 2 + 1) * LANES]
        own = (lane < MLA_V) if hd % 2 == 0 else (lane >= MLA_V)
        mv_ref[0, :, hd * LANES:(hd + 1) * LANES] = jnp.where(own, pair_tile, 1.0).astype(bf16)

    row = lax.broadcasted_iota(jnp.int32, (tm, LANES), 0) + pl.program_id(1) * tm
    kscale = jnp.where(row >= N_PAD, RET_QK ** -0.5, 0.0).astype(jnp.float32)
    rq = seg(C_RQ, C_RK)
    rk = seg(C_RK, C_RV)
    for grp in range(D_RQK // LANES):
        lo = grp * LANES
        rq_ref[0, :, lo:lo + LANES] = _rope_tile(rq[:, lo:lo + LANES], rtab_ref, RET_QK // 2).astype(bf16)
        rk_ref[0, :, lo:lo + LANES] = (_rope_tile(rk[:, lo:lo + LANES], rtab_ref, RET_QK // 2)
                                       * kscale).astype(bf16)
    rv_ref[0] = seg(C_RV, C_RG).astype(bf16)
    rg_ref[0] = seg(C_RG, N_IN_PAD).astype(bf16)


def _projections(h, win, qn, kvn, wuq, wuk, wuv, mtab, rtab):
    bsz, length, d = h.shape
    tm = _row_tile(length, 704)
    const2 = lambda i, j: (0, 0)
    tok = lambda w: pl.BlockSpec((1, tm, w), lambda i, j: (i, j, 0))
    widths = (D_SB, D_SB, D_SB, MLA_HEADS * MLA_QK_PAD, MLA_HEADS * MLA_QK_PAD, MLA_HEADS * LANES,
              D_RQK, D_RQK, D_RET, D_RET)
    return pl.pallas_call(
        functools.partial(_proj_kernel, tm=tm),
        grid=(bsz, length // tm),
        in_specs=[tok(d),
                  pl.BlockSpec(win.shape, const2),
                  pl.BlockSpec(qn.shape, const2),
                  pl.BlockSpec(kvn.shape, const2),
                  pl.BlockSpec(wuq.shape, const2),
                  pl.BlockSpec(wuk.shape, const2),
                  pl.BlockSpec(wuv.shape, const2),
                  pl.BlockSpec((3, tm, LANES), lambda i, j: (0, j, 0)),
                  pl.BlockSpec((3, tm, LANES), lambda i, j: (0, j, 0))],
        out_specs=[tok(w) for w in widths],
        out_shape=[jax.ShapeDtypeStruct((bsz, length, w), jnp.bfloat16) for w in widths],
        compiler_params=_cparams("parallel", "parallel"),
        name="projections",
    )(h, win, qn, kvn, wuq, wuk, wuv, mtab, rtab)


SB_STACK = 2 * BLOCK


def _sb_step(qstack, kbs, vbs, r_prev, mask, uo, head0):
    bf16 = jnp.bfloat16
    z = jnp.concatenate([_dot_nt(q, kb) for q, kb in zip(qstack, kbs)], axis=0)
    sp = jnp.log(1.0 + jnp.exp2(-jnp.abs(z))) * LOG2E
    lb = jnp.minimum(z, 0.0) - sp
    lk = lb - z
    if mask is not None:
        lk = jnp.where(mask, lk, 0.0)
    hi = lk.astype(bf16)
    mid = (lk - hi.astype(jnp.float32)).astype(bf16)
    ex = _dot(jnp.concatenate([hi, mid], axis=1), uo)
    arg = lb + ex[:, :LANES]
    if r_prev is not None:
        arg = arg + r_prev
    w = jnp.exp2(arg)
    if mask is not None:
        w = jnp.where(mask, w, 0.0)
    wb = w.astype(bf16)
    r_new = ex[:, LANES:] if r_prev is None else r_prev + ex[:, LANES:]
    outs = []
    for g, vb in enumerate(vbs):
        pv = _dot(wb[g * SB_STACK:(g + 1) * SB_STACK], vb)
        outs.append(jnp.where(head0, pv[:BLOCK], pv[BLOCK:]))
    return outs, r_new


def _sb_kernel(q_ref, k_ref, v_ref, uo_ref, o_ref, *, nblk, grp):
    uo = uo_ref[...]
    col1 = lax.broadcasted_iota(jnp.int32, (BLOCK, LANES), 1)
    head0 = col1 < SB_HEAD_DIM
    zero = jnp.zeros((BLOCK, LANES), jnp.bfloat16)
    rows_all = grp * SB_STACK
    col = lax.broadcasted_iota(jnp.int32, (rows_all, LANES), 1)
    row = lax.broadcasted_iota(jnp.int32, (rows_all, LANES), 0)
    stream = row // SB_STACK

    def per_stream(values):
        tile = jnp.full((rows_all, LANES), values[-1], jnp.int32)
        for g in range(grp - 2, -1, -1):
            tile = jnp.where(stream == g, values[g], tile)
        return tile

    def group(gi, carry):
        blocks = [gi * grp + g for g in range(grp)]
        rows = [pl.ds(pl.multiple_of(i * BLOCK, BLOCK), BLOCK) for i in blocks]
        qstack = []
        for g in range(grp):
            qb = q_ref[0, rows[g], :]
            qstack.append(jnp.concatenate([jnp.where(head0, qb, zero), jnp.where(head0, zero, qb)], axis=0))

        first_valid = per_stream([jnp.where(b == 0, N_PAD, 0) for b in blocks])
        mask = (col < (row % BLOCK)) & (col >= first_valid)
        outs, r = _sb_step(qstack, [k_ref[0, rw, :] for rw in rows], [v_ref[0, rw, :] for rw in rows],
                           None, mask, uo, head0)
        for g in range(grp):
            o_ref[0, rows[g], :] = outs[g]

        def step(c, with_mask):
            d, _, r_prev = c
            js = [b - d for b in blocks]
            krows = [pl.ds(pl.multiple_of(jnp.maximum(j, 0) * BLOCK, BLOCK), BLOCK) for j in js]
            mask = None
            if with_mask:
                fv = per_stream([jnp.where(j < 0, BLOCK, jnp.where(j == 0, N_PAD, 0)) for j in js])
                mask = col >= fv
            outs, r_new = _sb_step(qstack, [k_ref[0, rw, :] for rw in krows], [v_ref[0, rw, :] for rw in krows],
                                   r_prev, mask, uo, head0)
            for g in range(grp):
                o_ref[0, rows[g], :] += outs[g]
            if with_mask:
                need = jnp.float32(-jnp.inf)
                for g in range(grp):
                    rmax = jnp.max(r_new[g * SB_STACK:(g + 1) * SB_STACK])
                    need = jnp.maximum(need, jnp.where(js[g] >= 1, rmax, -jnp.inf))
            else:
                need = jnp.max(r_new)
            return d + 1, need > -SB_LOG2_CUTOFF, r_new

        c = lax.while_loop(lambda c: c[1] & (blocks[0] - c[0] >= 1), lambda c: step(c, False),
                           (jnp.int32(1), blocks[-1] >= 1, r))
        lax.while_loop(lambda c: c[1], lambda c: step(c, True), c)
        return carry

    lax.fori_loop(0, nblk // grp, group, 0)


def _stick_breaking(q, k, v, uo):
    bsz, length, _ = q.shape
    nblk = length // BLOCK
    grp = max(g for g in range(1, SB_GROUP_MAX + 1) if nblk % g == 0)
    pair = lambda: pl.BlockSpec((1, length, LANES), lambda b, p: (b, 0, p))
    return pl.pallas_call(
        functools.partial(_sb_kernel, nblk=nblk, grp=grp),
        grid=(bsz, D_SB // LANES),
        in_specs=[pair(), pair(), pair(), pl.BlockSpec(uo.shape, lambda b, p: (0, 0))],
        out_specs=pl.BlockSpec((1, length, LANES), lambda b, p: (b, 0, p)),
        out_shape=jax.ShapeDtypeStruct((bsz, length, D_SB), jnp.float32),
        compiler_params=_cparams("parallel", "parallel"),
        name="stick_breaking",
    )(q, k, v, uo)


def _mla_kernel(q_ref, k_ref, v_ref, o_ref, sa_ref, sb_ref, m_ref, acc_ref, *, tq):
    qi = pl.program_id(2)
    bf16 = jnp.bfloat16
    row = lax.broadcasted_iota(jnp.int32, (tq, tq), 0)
    col = lax.broadcasted_iota(jnp.int32, (tq, tq), 1)
    lane = lax.broadcasted_iota(jnp.int32, (tq, LANES), 1)
    col_first = lax.broadcasted_iota(jnp.int32, (tq, LANES), 1)
    qs = [q_ref[0, :, hd * MLA_QK_PAD:(hd + 1) * MLA_QK_PAD] for hd in range(2)]

    def key_rows(j):
        return pl.ds(pl.multiple_of(j * tq, tq), tq)

    def write_scores(buf, j, diagonal):
        if diagonal:
            mask = (col <= row) & ((col + qi * tq >= N_PAD) | (col == row))
        else:
            first_ok = col_first >= jnp.where(j == 0, N_PAD, 0)
        for hd in range(2):
            s = _dot_nt(qs[hd], k_ref[0, key_rows(j), hd * MLA_QK_PAD:(hd + 1) * MLA_QK_PAD])
            if diagonal:
                buf[hd] = jnp.where(mask, s, NEG_BIG)
            else:
                buf[hd, :, :LANES] = jnp.where(first_ok, s[:, :LANES], NEG_BIG)
                buf[hd, :, LANES:] = s[:, LANES:]

    def update(buf, j):
        for hd in range(2):
            m = m_ref[hd]
            s = buf[hd]
            m_new = jnp.maximum(m, jnp.max(s, axis=-1, keepdims=True))
            p = jnp.concatenate([jnp.exp2(s[:, c:c + LANES] - m_new) for c in range(0, tq, LANES)], axis=1)
            pv = _dot(p.astype(bf16), v_ref[0, key_rows(j), hd * LANES:(hd + 1) * LANES])
            acc_ref[hd] = jnp.exp2(m - m_new) * acc_ref[hd] + pv
            m_ref[hd] = m_new

    m_ref[...] = jnp.full(m_ref.shape, NEG_BIG, jnp.float32)
    acc_ref[...] = jnp.zeros(acc_ref.shape, jnp.float32)
    write_scores(sa_ref, qi, True)

    def pair(kk, carry):
        write_scores(sb_ref, 2 * kk, False)
        update(sa_ref, jnp.where(kk == 0, qi, 2 * kk - 1))

        @pl.when(2 * kk + 1 <= qi)
        def _():
            write_scores(sa_ref, 2 * kk + 1, False)
            update(sb_ref, 2 * kk)

        return carry

    lax.fori_loop(0, qi // 2 + 1, pair, 0)
    acc0, acc1 = acc_ref[0], acc_ref[1]
    n0 = acc0 / pltpu.roll(acc0, MLA_V, 1)
    n1 = acc1 / pltpu.roll(acc1, MLA_V, 1)
    o_ref[0] = jnp.where(lane < MLA_V, n0, n1).astype(o_ref.dtype)


def _mla_attention(q, k, v):
    bsz, length, _ = q.shape
    tq = _row_tile(length, 384)
    return pl.pallas_call(
        functools.partial(_mla_kernel, tq=tq),
        grid=(bsz, MLA_HEADS // 2, length // tq),
        in_specs=[pl.BlockSpec((1, tq, 2 * MLA_QK_PAD), lambda b, p, i: (b, i, p)),
                  pl.BlockSpec((1, length, 2 * MLA_QK_PAD), lambda b, p, i: (b, 0, p)),
                  pl.BlockSpec((1, length, 2 * LANES), lambda b, p, i: (b, 0, p))],
        out_specs=pl.BlockSpec((1, tq, LANES), lambda b, p, i: (b, i, p)),
        out_shape=jax.ShapeDtypeStruct((bsz, length, D_MLA), jnp.bfloat16),
        scratch_shapes=[pltpu.VMEM((2, tq, tq), jnp.float32), pltpu.VMEM((2, tq, tq), jnp.float32),
                        pltpu.VMEM((2, tq, LANES), jnp.float32), pltpu.VMEM((2, tq, LANES), jnp.float32)],
        compiler_params=_cparams("parallel", "parallel", "arbitrary"),
        name="mla_attention",
    )(q, k, v)


RET_UNROLL_MAX = 3


def _ret_kernel(q_ref, k_ref, v_ref, g_ref, dec_ref, o_ref, *, nblk):
    bf16 = jnp.bfloat16
    lane = lax.broadcasted_iota(jnp.int32, (BLOCK, LANES), 1)
    head_mask = (lane < RET_QK, lane >= RET_QK)
    zero = jnp.zeros((BLOCK, LANES), bf16)

    def chunks(ci, states):
        for u in range(unroll):
            states = chunk(ci * unroll + u, states)
        return states

    def chunk(c, states):
        rows = pl.ds(pl.multiple_of(c * BLOCK, BLOCK), BLOCK)
        qb = q_ref[0, rows, :]
        kb = k_ref[0, rows, :]
        new_states = []
        for hd in range(2):
            cols = slice(hd * RET_V, (hd + 1) * RET_V)
            d_in, q_decay, k_decay, c_decay = (dec_ref[hd, t] for t in range(4))
            qc = jnp.where(head_mask[hd], qb, zero)
            kc = jnp.where(head_mask[hd], kb, zero)
            vc = v_ref[0, rows, cols]
            state = states[hd]
            inner = _dot_nt(qc, kc) * d_in
            y = _dot(inner.astype(bf16), vc) + _dot(qc, state.astype(bf16)) * q_decay
            kd = (kc.astype(jnp.float32) * k_decay).astype(bf16)
            new_states.append(state * c_decay + _dot_tn(kd, vc))
            mu = jnp.mean(y, axis=-1, keepdims=True)
            yc = y - mu
            var = jnp.mean(yc * yc, axis=-1, keepdims=True)
            yn = yc * lax.rsqrt(var + LN_EPS)
            gate = g_ref[0, rows, cols].astype(jnp.float32)
            o_ref[0, rows, cols] = (gate * jax.nn.sigmoid(gate) * yn).astype(o_ref.dtype)
        return tuple(new_states)

    zeros = jnp.zeros((BLOCK, LANES), jnp.float32)
    unroll = max(u for u in range(1, RET_UNROLL_MAX + 1) if nblk % u == 0)
    lax.fori_loop(0, nblk // unroll, chunks, (zeros, zeros))


def _retention(q, k, v, g, dec):
    bsz, length, _ = v.shape
    qk = lambda: pl.BlockSpec((1, length, LANES), lambda b, p: (b, 0, p))
    vg = lambda: pl.BlockSpec((1, length, 2 * RET_V), lambda b, p: (b, 0, p))
    return pl.pallas_call(
        functools.partial(_ret_kernel, nblk=length // BLOCK),
        grid=(bsz, RET_HEADS // 2),
        in_specs=[qk(), qk(), vg(), vg(),
                  pl.BlockSpec((2, 4, BLOCK, LANES), lambda b, p: (p, 0, 0, 0))],
        out_specs=pl.BlockSpec((1, length, 2 * RET_V), lambda b, p: (b, 0, p)),
        out_shape=jax.ShapeDtypeStruct((bsz, length, D_RET), jnp.bfloat16),
        compiler_params=_cparams("parallel", "parallel"),
        name="retention",
    )(q, k, v, g, dec)


def _out_kernel(a_ref, b_ref, c_ref, h_ref, w_ref, g_ref, beta_ref, o_ref):
    bf16 = jnp.bfloat16
    mix = (_dot(a_ref[0].astype(bf16), w_ref[0:D_SB, :])
           + _dot(b_ref[0], w_ref[D_SB:D_SB + D_MLA, :])
           + _dot(c_ref[0], w_ref[D_SB + D_MLA:D_MIX, :]))
    o_ref[0] = _layer_norm_rows(DN_ALPHA * h_ref[0] + mix, g_ref[...], beta_ref[...])


def _out_projection(a, b, c, h, w, g, beta):
    bsz, length, d = h.shape
    tm = _row_tile(length, 704)
    tok = lambda wd: pl.BlockSpec((1, tm, wd), lambda i, j: (i, j, 0))
    const2 = lambda i, j: (0, 0)
    return pl.pallas_call(
        _out_kernel,
        grid=(bsz, length // tm),
        in_specs=[tok(D_SB), tok(D_MLA), tok(D_RET), tok(d),
                  pl.BlockSpec(w.shape, const2), pl.BlockSpec(g.shape, const2),
                  pl.BlockSpec(beta.shape, const2)],
        out_specs=tok(d),
        out_shape=jax.ShapeDtypeStruct(h.shape, jnp.float32),
        compiler_params=_cparams("parallel", "parallel"),
        name="out_projection",
    )(a, b, c, h, w, g, beta)


FF_CHUNK = 1024


def _mlp_kernel(h_ref, w1_ref, w2_ref, g_ref, beta_ref, o_ref):
    bf16 = jnp.bfloat16
    h = h_ref[0]
    hb = h.astype(bf16)
    acc = DN_ALPHA * h
    for lo in range(0, D_FF, FF_CHUNK):
        u = jnp.maximum(_dot(hb, w1_ref[:, lo:lo + FF_CHUNK]), 0.0)
        acc = acc + _dot((u * u).astype(bf16), w2_ref[lo:lo + FF_CHUNK, :])
    o_ref[0] = _layer_norm_rows(acc, g_ref[...], beta_ref[...])


def _mlp(h, w1, w2, g, beta):
    bsz, length, d = h.shape
    tm = _row_tile(length, 704)
    tok = pl.BlockSpec((1, tm, d), lambda i, j: (i, j, 0))
    const2 = lambda i, j: (0, 0)
    return pl.pallas_call(
        _mlp_kernel,
        grid=(bsz, length // tm),
        in_specs=[tok, pl.BlockSpec(w1.shape, const2), pl.BlockSpec(w2.shape, const2),
                  pl.BlockSpec(g.shape, const2), pl.BlockSpec(beta.shape, const2)],
        out_specs=tok,
        out_shape=jax.ShapeDtypeStruct(h.shape, jnp.float32),
        compiler_params=_cparams("parallel", "parallel"),
        name="mlp",
    )(h, w1, w2, g, beta)


def _rope_tables(length):
    pos = (jnp.arange(length) - N_PAD).astype(jnp.float32)

    def angles(half):
        inv = ROPE_THETA ** (-jnp.arange(half, dtype=jnp.float32) / half)
        ang = pos[:, None] * inv[None, :]
        return jnp.cos(ang), jnp.sin(ang)

    cos, sin = angles(MLA_ROPE // 2)
    one = jnp.ones((length, MLA_NOPE), jnp.float32)
    z = lambda w: jnp.zeros((length, w), jnp.float32)
    mla = jnp.stack([jnp.concatenate([one, cos, cos, z(32)], 1),
                     jnp.concatenate([z(64), -sin, z(16), z(32)], 1),
                     jnp.concatenate([z(64), z(16), sin, z(32)], 1)])
    cos, sin = angles(RET_QK // 2)
    ret = jnp.stack([jnp.concatenate([cos, cos] * 2, 1),
                     jnp.concatenate([-sin, z(32)] * 2, 1),
                     jnp.concatenate([z(32), sin] * 2, 1)])
    return mla, ret


def _retention_decays():
    log_g = jnp.log(jnp.array(RET_GAMMA, jnp.float32))
    idx = jnp.arange(BLOCK, dtype=jnp.float32)
    diff = idx[:, None] - idx[None, :]
    d_in = jnp.where(diff[None] >= 0, jnp.exp(jnp.maximum(diff, 0.0)[None] * log_g[:, None, None]), 0.0)
    q_decay = jnp.exp((idx[None, :] + 1.0) * log_g[:, None])
    k_decay = jnp.exp((BLOCK - 1.0 - idx[None, :]) * log_g[:, None])
    c_decay = jnp.exp(BLOCK * log_g)
    rep = lambda a: jnp.broadcast_to(a[:, :, None], (RET_HEADS, BLOCK, LANES))
    full = jnp.broadcast_to(c_decay[:, None, None], (RET_HEADS, BLOCK, LANES))
    return jnp.stack([d_in, rep(q_decay), rep(k_decay), full], axis=1)


def _suffix_sum_matrix():
    i = np.arange(BLOCK)
    upper = (i[:, None] > i[None, :]).astype(np.float32)
    ones = np.ones((BLOCK, BLOCK), np.float32)
    half = np.concatenate([upper, ones], axis=1)
    return jnp.asarray(np.concatenate([half, half], axis=0), jnp.bfloat16)


def kernel(x, meta_tokens, ln_emb_g, ln_emb_b, w_in, mla_q_norm, mla_kv_norm, w_uq, w_ukv, w_out,
           ln1_g, ln1_b, w_ff1, w_ff2, ln2_g, ln2_b):
    bsz, seq, d = x.shape
    assert d == D_MODEL
    bf16 = jnp.bfloat16
    depth = w_in.shape[0]

    meta = jnp.broadcast_to(meta_tokens[None].astype(x.dtype), (bsz, N_META, d))
    h = jnp.concatenate([jnp.zeros((bsz, N_PAD, d), x.dtype), meta, x], axis=1)
    length = h.shape[1]
    assert length % BLOCK == 0

    zc = lambda n: jnp.zeros((depth, d, n), w_in.dtype)
    k_r_lo = 3 * D_SB + MLA_Q_LORA + MLA_KV_LORA
    win = jnp.concatenate([w_in[..., :k_r_lo], zc(MLA_NOPE), w_in[..., k_r_lo:k_r_lo + MLA_ROPE],
                           zc(LANES - MLA_NOPE - MLA_ROPE), w_in[..., k_r_lo + MLA_ROPE:]], axis=-1).astype(bf16)
    assert win.shape[-1] == N_IN_PAD
    wuq = w_uq.reshape(depth, MLA_Q_LORA, MLA_HEADS, MLA_NOPE + MLA_ROPE)
    wuq = jnp.pad(wuq, ((0, 0), (0, 0), (0, 0), (0, MLA_QK_PAD - MLA_NOPE - MLA_ROPE)))
    wuq = wuq.reshape(depth, MLA_Q_LORA, MLA_HEADS * MLA_QK_PAD).astype(bf16)
    wukv = w_ukv.reshape(depth, MLA_KV_LORA, MLA_HEADS, MLA_NOPE + MLA_V)
    wuk = jnp.pad(wukv[..., :MLA_NOPE], ((0, 0), (0, 0), (0, 0), (0, MLA_QK_PAD - MLA_NOPE)))
    wuk = wuk.reshape(depth, MLA_KV_LORA, MLA_HEADS * MLA_QK_PAD).astype(bf16)
    wuv = wukv[..., MLA_NOPE:].reshape(depth, MLA_KV_LORA, D_MLA).astype(bf16)
    wout = w_out.astype(bf16)
    w1 = w_ff1.astype(bf16)
    w2 = w_ff2.astype(bf16)

    mtab, rtab = _rope_tables(length)
    dec = _retention_decays()
    uo = _suffix_sum_matrix()
    row2 = lambda a: a.reshape(1, -1)

    h = _embed_layer_norm(h, ln_emb_g, ln_emb_b)
    for l in range(depth):
        sbq, sbk, sbv, mq, mk, mv, rq, rk, rv, rg = _projections(
            h, win[l], row2(mla_q_norm[l]), row2(mla_kv_norm[l]), wuq[l], wuk[l], wuv[l], mtab, rtab)
        out_a = _stick_breaking(sbq, sbk, sbv, uo)
        out_b = _mla_attention(mq, mk, mv)
        out_c = _retention(rq, rk, rv, rg, dec)
        h = _out_projection(out_a, out_b, out_c, h, wout[l], row2(ln1_g[l]), row2(ln1_b[l]))
        h = _mlp(h, w1[l], w2[l], row2(ln2_g[l]), row2(ln2_b[l]))
    return h[:, N_PAD + N_META:]
```

```python
import functools

import numpy as np
import jax
import jax.numpy as jnp
from jax import lax
from jax.experimental import pallas as pl
from jax.experimental.pallas import tpu as pltpu

D_MODEL = 1024
DEPTH = 4
N_META = 16
BLOCK = 128
N_PAD = (-N_META) % BLOCK

SB_HEADS, SB_HEAD_DIM = 8, 64
MLA_HEADS, MLA_NOPE, MLA_ROPE, MLA_V = 8, 64, 32, 64
MLA_Q_LORA, MLA_KV_LORA = 384, 256
RET_HEADS, RET_QK, RET_V = 4, 64, 128

D_SB = SB_HEADS * SB_HEAD_DIM
D_MLA = MLA_HEADS * MLA_V
D_RET = RET_HEADS * RET_V
D_MIX = D_SB + D_MLA + D_RET
D_FF = 4 * D_MODEL
D_RQK = RET_HEADS * RET_QK

ROPE_THETA = 10000.0
LN_EPS = 1e-5
DN_ALPHA = (2 * DEPTH) ** 0.25
RET_GAMMA = tuple(1.0 - 2.0 ** (-5 - h) for h in range(RET_HEADS))

LANES = 128
MLA_QK_PAD = LANES

C_SBQ, C_SBK, C_SBV = 0, D_SB, 2 * D_SB
C_CQ = 3 * D_SB
C_CKV = C_CQ + MLA_Q_LORA
C_KR = C_CKV + MLA_KV_LORA
C_RQ = C_KR + LANES
C_RK = C_RQ + D_RQK
C_RV = C_RK + D_RQK
C_RG = C_RV + D_RET
N_IN_PAD = C_RG + D_RET

VMEM_LIMIT_BYTES = 56 * 1024 * 1024

LOG2E = 1.4426950408889634
SB_LOG_CUTOFF = 105.0
SB_LOG2_CUTOFF = SB_LOG_CUTOFF * LOG2E
SB_GROUP_MAX = 11
NEG_BIG = -1e30


def _row_tile(length, cap):
    best = None
    for t in range(16, min(length, cap) + 1, 16):
        if length % t == 0:
            best = t
    assert best is not None, length
    return best


def _cparams(*sem):
    return pltpu.CompilerParams(dimension_semantics=sem, vmem_limit_bytes=VMEM_LIMIT_BYTES)


def _layer_norm_rows(x, g, b):
    mu = jnp.mean(x, axis=-1, keepdims=True)
    xc = x - mu
    var = jnp.mean(xc * xc, axis=-1, keepdims=True)
    return xc * lax.rsqrt(var + LN_EPS) * g + b


def _rms_norm_rows(x, g):
    return x * lax.rsqrt(jnp.mean(x * x, axis=-1, keepdims=True) + LN_EPS) * g


def _dot(a, b):
    return jnp.dot(a, b, preferred_element_type=jnp.float32)


def _dot_nt(a, b):
    return lax.dot_general(a, b, (((1,), (1,)), ((), ())), preferred_element_type=jnp.float32)


def _dot_tn(a, b):
    return lax.dot_general(a, b, (((0,), (0,)), ((), ())), preferred_element_type=jnp.float32)


def _ln_kernel(x_ref, g_ref, b_ref, o_ref):
    o_ref[...] = _layer_norm_rows(x_ref[...], g_ref[...], b_ref[...])


def _embed_layer_norm(h, g, b):
    bsz, length, d = h.shape
    tm = _row_tile(length, 1408)
    return pl.pallas_call(
        _ln_kernel,
        grid=(bsz, length // tm),
        in_specs=[pl.BlockSpec((1, tm, d), lambda i, j: (i, j, 0)),
                  pl.BlockSpec((1, 1, d), lambda i, j: (0, 0, 0)),
                  pl.BlockSpec((1, 1, d), lambda i, j: (0, 0, 0))],
        out_specs=pl.BlockSpec((1, tm, d), lambda i, j: (i, j, 0)),
        out_shape=jax.ShapeDtypeStruct(h.shape, jnp.float32),
        compiler_params=_cparams("parallel", "parallel"),
        name="embed_ln",
    )(h, g.reshape(1, 1, d), b.reshape(1, 1, d))


def _rope_tile(x, tab_ref, half):
    c, s1, s2 = tab_ref[0], tab_ref[1], tab_ref[2]
    return x * c + pltpu.roll(x, LANES - half, 1) * s1 + pltpu.roll(x, half, 1) * s2


def _proj_kernel(h_ref, win_ref, qn_ref, kvn_ref, wuq_ref, wuk_ref, wuv_ref, mtab_ref, rtab_ref,
                 sbq_ref, sbk_ref, sbv_ref, mq_ref, mk_ref, mv_ref, rq_ref, rk_ref, rv_ref, rg_ref,
                 *, tm):
    bf16 = jnp.bfloat16
    hb = h_ref[0].astype(bf16)

    def seg(lo, hi):
        return _dot(hb, win_ref[:, lo:hi])

    sbq_ref[0] = (seg(C_SBQ, C_SBK) * (LOG2E * SB_HEAD_DIM ** -0.5)).astype(bf16)
    sbk_ref[0] = seg(C_SBK, C_SBV).astype(bf16)
    sbv_ref[0] = seg(C_SBV, C_CQ).astype(bf16)

    cq = _rms_norm_rows(seg(C_CQ, C_CKV), qn_ref[...])
    q = _dot(cq.astype(bf16), wuq_ref[...])
    mla_scale = LOG2E * (MLA_NOPE + MLA_ROPE) ** -0.5
    for hd in range(MLA_HEADS):
        lo = hd * MLA_QK_PAD
        blk = _rope_tile(q[:, lo:lo + MLA_QK_PAD], mtab_ref, MLA_ROPE // 2)
        mq_ref[0, :, lo:lo + MLA_QK_PAD] = (blk * mla_scale).astype(bf16)
    ckv = _rms_norm_rows(seg(C_CKV, C_KR), kvn_ref[...]).astype(bf16)
    kn = _dot(ckv, wuk_ref[...])
    kr = _rope_tile(seg(C_KR, C_RQ), mtab_ref, MLA_ROPE // 2)
    for hd in range(MLA_HEADS):
        lo = hd * MLA_QK_PAD
        mk_ref[0, :, lo:lo + MLA_QK_PAD] = (kn[:, lo:lo + MLA_QK_PAD] + kr).astype(bf16)
    mv = _dot(ckv, wuv_ref[...])
    lane = lax.broadcasted_iota(jnp.int32, (tm, LANES), 1)
    for hd in range(MLA_HEADS):
        pair_tile = mv[:, (hd // 2) * LANES:(hd // 2 + 1) * LANES]
        own = (lane < MLA_V) if hd % 2 == 0 else (lane >= MLA_V)
        mv_ref[0, :, hd * LANES:(hd + 1) * LANES] = jnp.where(own, pair_tile, 1.0).astype(bf16)

    row = lax.broadcasted_iota(jnp.int32, (tm, LANES), 0) + pl.program_id(1) * tm
    kscale = jnp.where(row >= N_PAD, RET_QK ** -0.5, 0.0).astype(jnp.float32)
    rq = seg(C_RQ, C_RK)
    rk = seg(C_RK, C_RV)
    for grp in range(D_RQK // LANES):
        lo = grp * LANES
        rq_ref[0, :, lo:lo + LANES] = _rope_tile(rq[:, lo:lo + LANES], rtab_ref, RET_QK // 2).astype(bf16)
        rk_ref[0, :, lo:lo + LANES] = (_rope_tile(rk[:, lo:lo + LANES], rtab_ref, RET_QK // 2)
                                       * kscale).astype(bf16)
    rv_ref[0] = seg(C_RV, C_RG).astype(bf16)
    rg_ref[0] = seg(C_RG, N_IN_PAD).astype(bf16)


def _projections(h, win, qn, kvn, wuq, wuk, wuv, mtab, rtab):
    bsz, length, d = h.shape
    tm = _row_tile(length, 704)
    const2 = lambda i, j: (0, 0)
    tok = lambda w: pl.BlockSpec((1, tm, w), lambda i, j: (i, j, 0))
    widths = (D_SB, D_SB, D_SB, MLA_HEADS * MLA_QK_PAD, MLA_HEADS * MLA_QK_PAD, MLA_HEADS * LANES,
              D_RQK, D_RQK, D_RET, D_RET)
    return pl.pallas_call(
        functools.partial(_proj_kernel, tm=tm),
        grid=(bsz, length // tm),
        in_specs=[tok(d),
                  pl.BlockSpec(win.shape, const2),
                  pl.BlockSpec(qn.shape, const2),
                  pl.BlockSpec(kvn.shape, const2),
                  pl.BlockSpec(wuq.shape, const2),
                  pl.BlockSpec(wuk.shape, const2),
                  pl.BlockSpec(wuv.shape, const2),
                  pl.BlockSpec((3, tm, LANES), lambda i, j: (0, j, 0)),
                  pl.BlockSpec((3, tm, LANES), lambda i, j: (0, j, 0))],
        out_specs=[tok(w) for w in widths],
        out_shape=[jax.ShapeDtypeStruct((bsz, length, w), jnp.bfloat16) for w in widths],
        compiler_params=_cparams("parallel", "parallel"),
        name="projections",
    )(h, win, qn, kvn, wuq, wuk, wuv, mtab, rtab)


SB_STACK = 2 * BLOCK


def _sb_step(qstack, kbs, vbs, r_prev, mask, uo, head0):
    bf16 = jnp.bfloat16
    z = jnp.concatenate([_dot_nt(q, kb) for q, kb in zip(qstack, kbs)], axis=0)
    sp = jnp.log(1.0 + jnp.exp2(-jnp.abs(z))) * LOG2E
    lb = jnp.minimum(z, 0.0) - sp
    lk = lb - z
    if mask is not None:
        lk = jnp.where(mask, lk, 0.0)
    hi = lk.astype(bf16)
    mid = (lk - hi.astype(jnp.float32)).astype(bf16)
    ex = _dot(jnp.concatenate([hi, mid], axis=1), uo)
    arg = lb + ex[:, :LANES]
    if r_prev is not None:
        arg = arg + r_prev
    w = jnp.exp2(arg)
    if mask is not None:
        w = jnp.where(mask, w, 0.0)
    wb = w.astype(bf16)
    r_new = ex[:, LANES:] if r_prev is None else r_prev + ex[:, LANES:]
    outs = []
    for g, vb in enumerate(vbs):
        pv = _dot(wb[g * SB_STACK:(g + 1) * SB_STACK], vb)
        outs.append(jnp.where(head0, pv[:BLOCK], pv[BLOCK:]))
    return outs, r_new


def _sb_kernel(q_ref, k_ref, v_ref, uo_ref, o_ref, *, nblk, grp):
    uo = uo_ref[...]
    col1 = lax.broadcasted_iota(jnp.int32, (BLOCK, LANES), 1)
    head0 = col1 < SB_HEAD_DIM
    zero = jnp.zeros((BLOCK, LANES), jnp.bfloat16)
    rows_all = grp * SB_STACK
    col = lax.broadcasted_iota(jnp.int32, (rows_all, LANES), 1)
    row = lax.broadcasted_iota(jnp.int32, (rows_all, LANES), 0)
    stream = row // SB_STACK

    def per_stream(values):
        tile = jnp.full((rows_all, LANES), values[-1], jnp.int32)
        for g in range(grp - 2, -1, -1):
            tile = jnp.where(stream == g, values[g], tile)
        return tile

    def group(gi, carry):
        blocks = [gi * grp + g for g in range(grp)]
        rows = [pl.ds(pl.multiple_of(i * BLOCK, BLOCK), BLOCK) for i in blocks]
        qstack = []
        for g in range(grp):
            qb = q_ref[0, rows[g], :]
            qstack.append(jnp.concatenate([jnp.where(head0, qb, zero), jnp.where(head0, zero, qb)], axis=0))

        first_valid = per_stream([jnp.where(b == 0, N_PAD, 0) for b in blocks])
        mask = (col < (row % BLOCK)) & (col >= first_valid)
        outs, r = _sb_step(qstack, [k_ref[0, rw, :] for rw in rows], [v_ref[0, rw, :] for rw in rows],
                           None, mask, uo, head0)
        for g in range(grp):
            o_ref[0, rows[g], :] = outs[g]

        def step(c, with_mask):
            d, _, r_prev = c
            js = [b - d for b in blocks]
            krows = [pl.ds(pl.multiple_of(jnp.maximum(j, 0) * BLOCK, BLOCK), BLOCK) for j in js]
            mask = None
            if with_mask:
                fv = per_stream([jnp.where(j < 0, BLOCK, jnp.where(j == 0, N_PAD, 0)) for j in js])
                mask = col >= fv
            outs, r_new = _sb_step(qstack, [k_ref[0, rw, :] for rw in krows], [v_ref[0, rw, :] for rw in krows],
                                   r_prev, mask, uo, head0)
            for g in range(grp):
                o_ref[0, rows[g], :] += outs[g]
            if with_mask:
                need = jnp.float32(-jnp.inf)
                for g in range(grp):
                    rmax = jnp.max(r_new[g * SB_STACK:(g + 1) * SB_STACK])
                    need = jnp.maximum(need, jnp.where(js[g] >= 1, rmax, -jnp.inf))
            else:
                need = jnp.max(r_new)
            return d + 1, need > -SB_LOG2_CUTOFF, r_new

        c = lax.while_loop(lambda c: c[1] & (blocks[0] - c[0] >= 1), lambda c: step(c, False),
                           (jnp.int32(1), blocks[-1] >= 1, r))
        lax.while_loop(lambda c: c[1], lambda c: step(c, True), c)
        return carry

    lax.fori_loop(0, nblk // grp, group, 0)


def _stick_breaking(q, k, v, uo):
    bsz, length, _ = q.shape
    nblk = length // BLOCK
    grp = max(g for g in range(1, SB_GROUP_MAX + 1) if nblk % g == 0)
    pair = lambda: pl.BlockSpec((1, length, LANES), lambda b, p: (b, 0, p))
    return pl.pallas_call(
        functools.partial(_sb_kernel, nblk=nblk, grp=grp),
        grid=(bsz, D_SB // LANES),
        in_specs=[pair(), pair(), pair(), pl.BlockSpec(uo.shape, lambda b, p: (0, 0))],
        out_specs=pl.BlockSpec((1, length, LANES), lambda b, p: (b, 0, p)),
        out_shape=jax.ShapeDtypeStruct((bsz, length, D_SB), jnp.float32),
        compiler_params=_cparams("parallel", "parallel"),
        name="stick_breaking",
    )(q, k, v, uo)


MLA_BIAS_DIAG, MLA_BIAS_DIAG_FIRST, MLA_BIAS_PAD, MLA_BIAS_NONE = range(4)
MLA_UNROLL_MAX = 6


def _mla_bias_table(tq):
    row = np.arange(tq)[:, None]
    col = np.arange(tq)[None, :]
    keep = np.stack([col <= row,
                     (col <= row) & ((col >= N_PAD) | (col == row)),
                     np.broadcast_to(col >= N_PAD, (tq, tq)),
                     np.ones((tq, tq), bool)])
    return jnp.asarray(np.where(keep, 0.0, NEG_BIG), jnp.float32)


def _mla_kernel(q_ref, k_ref, v_ref, bias_ref, o_ref, sa_ref, sb_ref, m_ref, acc_ref, *, tq, nq):
    bf16 = jnp.bfloat16
    lane = lax.broadcasted_iota(jnp.int32, (tq, LANES), 1)
    n_steps = nq * (nq + 1) // 2
    assert n_steps % 2 == 0, "the two logits buffers alternate, so steps are taken in pairs"
    unroll = max(u for u in range(2, MLA_UNROLL_MAX + 1, 2) if n_steps % u == 0)

    def rows(i):
        return pl.ds(pl.multiple_of(i * tq, tq), tq)

    def key_tile(qi, pos):
        return jnp.where(pos == 0, qi, pos - 1)

    def advance(qi, pos):
        last = pos == qi
        return jnp.where(last, jnp.minimum(qi + 1, nq - 1), qi), jnp.where(last, 0, pos + 1)

    def write_scores(buf, qi, pos):
        kt = key_tile(qi, pos)
        kind = jnp.where(pos == 0, jnp.where(qi == 0, MLA_BIAS_DIAG_FIRST, MLA_BIAS_DIAG),
                         jnp.where(kt == 0, MLA_BIAS_PAD, MLA_BIAS_NONE))
        for hd in range(2):
            cols = slice(hd * MLA_QK_PAD, (hd + 1) * MLA_QK_PAD)
            buf[hd] = _dot_nt(q_ref[0, rows(qi), cols], k_ref[0, rows(kt), cols]) + bias_ref[kind]

    def update(buf, qi, pos):
        kt = key_tile(qi, pos)
        for hd in range(2):
            m = jnp.where(pos == 0, NEG_BIG, m_ref[hd])
            s = buf[hd]
            m_new = jnp.maximum(m, jnp.max(s, axis=-1, keepdims=True))
            p = jnp.concatenate([jnp.exp2(s[:, c:c + LANES] - m_new) for c in range(0, tq, LANES)], axis=1)
            pv = _dot(p.astype(bf16), v_ref[0, rows(kt), hd * LANES:(hd + 1) * LANES])
            acc_ref[hd, rows(qi)] = jnp.exp2(m - m_new) * acc_ref[hd, rows(qi)] + pv
            m_ref[hd] = m_new

    acc_ref[...] = jnp.zeros(acc_ref.shape, jnp.float32)
    m_ref[...] = jnp.full(m_ref.shape, NEG_BIG, jnp.float32)
    write_scores(sa_ref, 0, 0)

    def steps(i, cur):
        for _ in range(unroll // 2):
            nxt = advance(*cur)
            write_scores(sb_ref, *nxt)
            update(sa_ref, *cur)
            cur = advance(*nxt)
            write_scores(sa_ref, *cur)
            update(sb_ref, *nxt)
        return cur

    lax.fori_loop(0, n_steps // unroll, steps, (jnp.int32(0), jnp.int32(0)))

    def normalize(qi, carry):
        acc0, acc1 = acc_ref[0, rows(qi)], acc_ref[1, rows(qi)]
        n0 = acc0 / pltpu.roll(acc0, MLA_V, 1)
        n1 = acc1 / pltpu.roll(acc1, MLA_V, 1)
        o_ref[0, rows(qi), :] = jnp.where(lane < MLA_V, n0, n1).astype(o_ref.dtype)
        return carry

    lax.fori_loop(0, nq, normalize, 0)


def _mla_attention(q, k, v):
    bsz, length, _ = q.shape
    tq = _row_tile(length, 384)
    bias = _mla_bias_table(tq)
    seq = lambda w: pl.BlockSpec((1, length, w), lambda b, p: (b, 0, p))
    return pl.pallas_call(
        functools.partial(_mla_kernel, tq=tq, nq=length // tq),
        grid=(bsz, MLA_HEADS // 2),
        in_specs=[seq(2 * MLA_QK_PAD), seq(2 * MLA_QK_PAD), seq(2 * LANES),
                  pl.BlockSpec(bias.shape, lambda b, p: (0, 0, 0))],
        out_specs=seq(LANES),
        out_shape=jax.ShapeDtypeStruct((bsz, length, D_MLA), jnp.bfloat16),
        scratch_shapes=[pltpu.VMEM((2, tq, tq), jnp.float32), pltpu.VMEM((2, tq, tq), jnp.float32),
                        pltpu.VMEM((2, tq, LANES), jnp.float32), pltpu.VMEM((2, length, LANES), jnp.float32)],
        compiler_params=_cparams("parallel", "parallel"),
        name="mla_attention",
    )(q, k, v, bias)


RET_UNROLL_MAX = 11


def _ret_kernel(q_ref, k_ref, v_ref, g_ref, dec_ref, o_ref, *, nblk):
    bf16 = jnp.bfloat16
    f32 = jnp.float32
    lane = lax.broadcasted_iota(jnp.int32, (BLOCK, LANES), 1)
    lane2 = lax.broadcasted_iota(jnp.int32, (BLOCK, 2 * RET_V), 1)
    feat = lax.broadcasted_iota(jnp.int32, (BLOCK, 2 * RET_V), 0)
    zero = jnp.zeros((BLOCK, LANES), bf16)
    zero2 = jnp.zeros((BLOCK, 2 * RET_V), bf16)
    d_in = jnp.concatenate([dec_ref[0, 0], dec_ref[1, 0]], axis=0)
    q_decay = jnp.concatenate([dec_ref[0, 1], dec_ref[1, 1]], axis=1)
    k_decay = jnp.where(lane < RET_QK, dec_ref[0, 2], dec_ref[1, 2])
    c_decay = jnp.concatenate([dec_ref[0, 3], dec_ref[1, 3]], axis=1)
    same_head = ((feat < RET_QK) == (lane2 < RET_V)).astype(f32)

    def chunks(ci, state):
        for u in range(unroll):
            state = chunk(ci * unroll + u, state)
        return state

    def chunk(c, state):
        rows = pl.ds(pl.multiple_of(c * BLOCK, BLOCK), BLOCK)
        qb = q_ref[0, rows, :]
        kb = k_ref[0, rows, :]
        vt = v_ref[0, rows, :]
        qstack = jnp.concatenate([jnp.where(lane < RET_QK, qb, zero), jnp.where(lane < RET_QK, zero, qb)], axis=0)
        inner = _dot_nt(qstack, kb) * d_in
        inner_cat = jnp.concatenate([inner[:BLOCK], inner[BLOCK:]], axis=1).astype(bf16)
        v_blocks = jnp.concatenate([jnp.where(lane2 < RET_V, vt, zero2), jnp.where(lane2 < RET_V, zero2, vt)], axis=0)
        y = _dot(inner_cat, v_blocks) + _dot(qb, state.astype(bf16)) * q_decay
        kd = (kb.astype(f32) * k_decay).astype(bf16)
        state = state * c_decay + _dot_tn(kd, vt) * same_head
        for hd in range(2):
            cols = slice(hd * RET_V, (hd + 1) * RET_V)
            yh = y[:, cols]
            mu = jnp.mean(yh, axis=-1, keepdims=True)
            yc = yh - mu
            var = jnp.mean(yc * yc, axis=-1, keepdims=True)
            yn = yc * lax.rsqrt(var + LN_EPS)
            gate = g_ref[0, rows, cols].astype(f32)
            o_ref[0, rows, cols] = (gate * jax.nn.sigmoid(gate) * yn).astype(o_ref.dtype)
        return state

    unroll = max(u for u in range(1, RET_UNROLL_MAX + 1) if nblk % u == 0)
    lax.fori_loop(0, nblk // unroll, chunks, jnp.zeros((BLOCK, 2 * RET_V), f32))


def _retention(q, k, v, g, dec):
    bsz, length, _ = v.shape
    qk = lambda: pl.BlockSpec((1, length, LANES), lambda b, p: (b, 0, p))
    vg = lambda: pl.BlockSpec((1, length, 2 * RET_V), lambda b, p: (b, 0, p))
    return pl.pallas_call(
        functools.partial(_ret_kernel, nblk=length // BLOCK),
        grid=(bsz, RET_HEADS // 2),
        in_specs=[qk(), qk(), vg(), vg(),
                  pl.BlockSpec((2, 4, BLOCK, LANES), lambda b, p: (p, 0, 0, 0))],
        out_specs=pl.BlockSpec((1, length, 2 * RET_V), lambda b, p: (b, 0, p)),
        out_shape=jax.ShapeDtypeStruct((bsz, length, D_RET), jnp.bfloat16),
        compiler_params=_cparams("parallel", "parallel"),
        name="retention",
    )(q, k, v, g, dec)


def _out_kernel(a_ref, b_ref, c_ref, h_ref, w_ref, g_ref, beta_ref, o_ref):
    bf16 = jnp.bfloat16
    mix = (_dot(a_ref[0].astype(bf16), w_ref[0:D_SB, :])
           + _dot(b_ref[0], w_ref[D_SB:D_SB + D_MLA, :])
           + _dot(c_ref[0], w_ref[D_SB + D_MLA:D_MIX, :]))
    o_ref[0] = _layer_norm_rows(DN_ALPHA * h_ref[0] + mix, g_ref[...], beta_ref[...])


def _out_projection(a, b, c, h, w, g, beta):
    bsz, length, d = h.shape
    tm = _row_tile(length, 704)
    tok = lambda wd: pl.BlockSpec((1, tm, wd), lambda i, j: (i, j, 0))
    const2 = lambda i, j: (0, 0)
    return pl.pallas_call(
        _out_kernel,
        grid=(bsz, length // tm),
        in_specs=[tok(D_SB), tok(D_MLA), tok(D_RET), tok(d),
                  pl.BlockSpec(w.shape, const2), pl.BlockSpec(g.shape, const2),
                  pl.BlockSpec(beta.shape, const2)],
        out_specs=tok(d),
        out_shape=jax.ShapeDtypeStruct(h.shape, jnp.float32),
        compiler_params=_cparams("parallel", "parallel"),
        name="out_projection",
    )(a, b, c, h, w, g, beta)


FF_CHUNK = 1024


def _mlp_kernel(h_ref, w1_ref, w2_ref, g_ref, beta_ref, o_ref):
    bf16 = jnp.bfloat16
    h = h_ref[0]
    hb = h.astype(bf16)
    acc = DN_ALPHA * h
    for lo in range(0, D_FF, FF_CHUNK):
        u = jnp.maximum(_dot(hb, w1_ref[:, lo:lo + FF_CHUNK]), 0.0)
        acc = acc + _dot((u * u).astype(bf16), w2_ref[lo:lo + FF_CHUNK, :])
    o_ref[0] = _layer_norm_rows(acc, g_ref[...], beta_ref[...])


def _mlp(h, w1, w2, g, beta):
    bsz, length, d = h.shape
    tm = _row_tile(length, 704)
    tok = pl.BlockSpec((1, tm, d), lambda i, j: (i, j, 0))
    const2 = lambda i, j: (0, 0)
    return pl.pallas_call(
        _mlp_kernel,
        grid=(bsz, length // tm),
        in_specs=[tok, pl.BlockSpec(w1.shape, const2), pl.BlockSpec(w2.shape, const2),
                  pl.BlockSpec(g.shape, const2), pl.BlockSpec(beta.shape, const2)],
        out_specs=tok,
        out_shape=jax.ShapeDtypeStruct(h.shape, jnp.float32),
        compiler_params=_cparams("parallel", "parallel"),
        name="mlp",
    )(h, w1, w2, g, beta)


def _rope_tables(length):
    pos = (jnp.arange(length) - N_PAD).astype(jnp.float32)

    def angles(half):
        inv = ROPE_THETA ** (-jnp.arange(half, dtype=jnp.float32) / half)
        ang = pos[:, None] * inv[None, :]
        return jnp.cos(ang), jnp.sin(ang)

    cos, sin = angles(MLA_ROPE // 2)
    one = jnp.ones((length, MLA_NOPE), jnp.float32)
    z = lambda w: jnp.zeros((length, w), jnp.float32)
    mla = jnp.stack([jnp.concatenate([one, cos, cos, z(32)], 1),
                     jnp.concatenate([z(64), -sin, z(16), z(32)], 1),
                     jnp.concatenate([z(64), z(16), sin, z(32)], 1)])
    cos, sin = angles(RET_QK // 2)
    ret = jnp.stack([jnp.concatenate([cos, cos] * 2, 1),
                     jnp.concatenate([-sin, z(32)] * 2, 1),
                     jnp.concatenate([z(32), sin] * 2, 1)])
    return mla, ret


def _retention_decays():
    log_g = jnp.log(jnp.array(RET_GAMMA, jnp.float32))
    idx = jnp.arange(BLOCK, dtype=jnp.float32)
    diff = idx[:, None] - idx[None, :]
    d_in = jnp.where(diff[None] >= 0, jnp.exp(jnp.maximum(diff, 0.0)[None] * log_g[:, None, None]), 0.0)
    q_decay = jnp.exp((idx[None, :] + 1.0) * log_g[:, None])
    k_decay = jnp.exp((BLOCK - 1.0 - idx[None, :]) * log_g[:, None])
    c_decay = jnp.exp(BLOCK * log_g)
    rep = lambda a: jnp.broadcast_to(a[:, :, None], (RET_HEADS, BLOCK, LANES))
    full = jnp.broadcast_to(c_decay[:, None, None], (RET_HEADS, BLOCK, LANES))
    return jnp.stack([d_in, rep(q_decay), rep(k_decay), full], axis=1)


def _suffix_sum_matrix():
    i = np.arange(BLOCK)
    upper = (i[:, None] > i[None, :]).astype(np.float32)
    ones = np.ones((BLOCK, BLOCK), np.float32)
    half = np.concatenate([upper, ones], axis=1)
    return jnp.asarray(np.concatenate([half, half], axis=0), jnp.bfloat16)


def kernel(x, meta_tokens, ln_emb_g, ln_emb_b, w_in, mla_q_norm, mla_kv_norm, w_uq, w_ukv, w_out,
           ln1_g, ln1_b, w_ff1, w_ff2, ln2_g, ln2_b):
    bsz, seq, d = x.shape
    assert d == D_MODEL
    bf16 = jnp.bfloat16
    depth = w_in.shape[0]

    meta = jnp.broadcast_to(meta_tokens[None].astype(x.dtype), (bsz, N_META, d))
    h = jnp.concatenate([jnp.zeros((bsz, N_PAD, d), x.dtype), meta, x], axis=1)
    length = h.shape[1]
    assert length % BLOCK == 0

    zc = lambda n: jnp.zeros((depth, d, n), w_in.dtype)
    k_r_lo = 3 * D_SB + MLA_Q_LORA + MLA_KV_LORA
    win = jnp.concatenate([w_in[..., :k_r_lo], zc(MLA_NOPE), w_in[..., k_r_lo:k_r_lo + MLA_ROPE],
                           zc(LANES - MLA_NOPE - MLA_ROPE), w_in[..., k_r_lo + MLA_ROPE:]], axis=-1).astype(bf16)
    assert win.shape[-1] == N_IN_PAD
    wuq = w_uq.reshape(depth, MLA_Q_LORA, MLA_HEADS, MLA_NOPE + MLA_ROPE)
    wuq = jnp.pad(wuq, ((0, 0), (0, 0), (0, 0), (0, MLA_QK_PAD - MLA_NOPE - MLA_ROPE)))
    wuq = wuq.reshape(depth, MLA_Q_LORA, MLA_HEADS * MLA_QK_PAD).astype(bf16)
    wukv = w_ukv.reshape(depth, MLA_KV_LORA, MLA_HEADS, MLA_NOPE + MLA_V)
    wuk = jnp.pad(wukv[..., :MLA_NOPE], ((0, 0), (0, 0), (0, 0), (0, MLA_QK_PAD - MLA_NOPE)))
    wuk = wuk.reshape(depth, MLA_KV_LORA, MLA_HEADS * MLA_QK_PAD).astype(bf16)
    wuv = wukv[..., MLA_NOPE:].reshape(depth, MLA_KV_LORA, D_MLA).astype(bf16)
    wout = w_out.astype(bf16)
    w1 = w_ff1.astype(bf16)
    w2 = w_ff2.astype(bf16)

    mtab, rtab = _rope_tables(length)
    dec = _retention_decays()
    uo = _suffix_sum_matrix()
    row2 = lambda a: a.reshape(1, -1)

    h = _embed_layer_norm(h, ln_emb_g, ln_emb_b)
    for l in range(depth):
        sbq, sbk, sbv, mq, mk, mv, rq, rk, rv, rg = _projections(
            h, win[l], row2(mla_q_norm[l]), row2(mla_kv_norm[l]), wuq[l], wuk[l], wuv[l], mtab, rtab)
        out_a = _stick_breaking(sbq, sbk, sbv, uo)
        out_b = _mla_attention(mq, mk, mv)
        out_c = _retention(rq, rk, rv, rg, dec)
        h = _out_projection(out_a, out_b, out_c, h, wout[l], row2(ln1_g[l]), row2(ln1_b[l]))
        h = _mlp(h, w1[l], w2[l], row2(ln2_g[l]), row2(ln2_b[l]))
    return h[:, N_PAD + N_META:]
```

```python
import functools

import numpy as np
import jax
import jax.numpy as jnp
from jax import lax
from jax.experimental import pallas as pl
from jax.experimental.pallas import tpu as pltpu

D_MODEL = 1024
DEPTH = 4
N_META = 16
BLOCK = 128
N_PAD = (-N_META) % BLOCK

SB_HEADS, SB_HEAD_DIM = 8, 64
MLA_HEADS, MLA_NOPE, MLA_ROPE, MLA_V = 8, 64, 32, 64
MLA_Q_LORA, MLA_KV_LORA = 384, 256
RET_HEADS, RET_QK, RET_V = 4, 64, 128

D_SB = SB_HEADS * SB_HEAD_DIM
D_MLA = MLA_HEADS * MLA_V
D_RET = RET_HEADS * RET_V
D_MIX = D_SB + D_MLA + D_RET
D_FF = 4 * D_MODEL
D_RQK = RET_HEADS * RET_QK

ROPE_THETA = 10000.0
LN_EPS = 1e-5
DN_ALPHA = (2 * DEPTH) ** 0.25
RET_GAMMA = tuple(1.0 - 2.0 ** (-5 - h) for h in range(RET_HEADS))

LANES = 128
MLA_QK_PAD = LANES

C_SBQ, C_SBK, C_SBV = 0, D_SB, 2 * D_SB
C_CQ = 3 * D_SB
C_CKV = C_CQ + MLA_Q_LORA
C_KR = C_CKV + MLA_KV_LORA
C_RQ = C_KR + LANES
C_RK = C_RQ + D_RQK
C_RV = C_RK + D_RQK
C_RG = C_RV + D_RET
N_IN_PAD = C_RG + D_RET

VMEM_LIMIT_BYTES = 56 * 1024 * 1024

LOG2E = 1.4426950408889634
SB_LOG_CUTOFF = 105.0
SB_LOG2_CUTOFF = SB_LOG_CUTOFF * LOG2E
SB_GROUP_MAX = 11
NEG_BIG = -1e30


def _row_tile(length, cap):
    best = None
    for t in range(16, min(length, cap) + 1, 16):
        if length % t == 0:
            best = t
    assert best is not None, length
    return best


def _cparams(*sem):
    return pltpu.CompilerParams(dimension_semantics=sem, vmem_limit_bytes=VMEM_LIMIT_BYTES)


def _layer_norm_rows(x, g, b):
    mu = jnp.mean(x, axis=-1, keepdims=True)
    xc = x - mu
    var = jnp.mean(xc * xc, axis=-1, keepdims=True)
    return xc * lax.rsqrt(var + LN_EPS) * g + b


def _rms_norm_rows(x, g):
    return x * lax.rsqrt(jnp.mean(x * x, axis=-1, keepdims=True) + LN_EPS) * g


def _dot(a, b):
    return jnp.dot(a, b, preferred_element_type=jnp.float32)


def _dot_nt(a, b):
    return lax.dot_general(a, b, (((1,), (1,)), ((), ())), preferred_element_type=jnp.float32)


def _dot_tn(a, b):
    return lax.dot_general(a, b, (((0,), (0,)), ((), ())), preferred_element_type=jnp.float32)


def _ln_kernel(x_ref, g_ref, b_ref, o_ref):
    o_ref[...] = _layer_norm_rows(x_ref[...], g_ref[...], b_ref[...])


def _embed_layer_norm(h, g, b):
    bsz, length, d = h.shape
    tm = _row_tile(length, 1408)
    return pl.pallas_call(
        _ln_kernel,
        grid=(bsz, length // tm),
        in_specs=[pl.BlockSpec((1, tm, d), lambda i, j: (i, j, 0)),
                  pl.BlockSpec((1, 1, d), lambda i, j: (0, 0, 0)),
                  pl.BlockSpec((1, 1, d), lambda i, j: (0, 0, 0))],
        out_specs=pl.BlockSpec((1, tm, d), lambda i, j: (i, j, 0)),
        out_shape=jax.ShapeDtypeStruct(h.shape, jnp.float32),
        compiler_params=_cparams("parallel", "parallel"),
        name="embed_ln",
    )(h, g.reshape(1, 1, d), b.reshape(1, 1, d))


def _rope_tile(x, tab_ref, half):
    c, s1, s2 = tab_ref[0], tab_ref[1], tab_ref[2]
    return x * c + pltpu.roll(x, LANES - half, 1) * s1 + pltpu.roll(x, half, 1) * s2


def _proj_kernel(h_ref, win_ref, qn_ref, kvn_ref, wuq_ref, wuk_ref, wuv_ref, mtab_ref, rtab_ref,
                 sbq_ref, sbk_ref, sbv_ref, mq_ref, mk_ref, mv_ref, rq_ref, rk_ref, rv_ref, rg_ref,
                 *, tm):
    bf16 = jnp.bfloat16
    hb = h_ref[0].astype(bf16)

    groups = {}

    def seg(lo, hi):
        for g_lo, g_hi in ((C_SBQ, C_CQ), (C_CQ, C_RQ), (C_RQ, N_IN_PAD)):
            if g_lo <= lo and hi <= g_hi:
                if g_lo not in groups:
                    groups[g_lo] = _dot(hb, win_ref[:, g_lo:g_hi])
                return groups[g_lo][:, lo - g_lo:hi - g_lo]
        raise ValueError((lo, hi))

    sbq_ref[0] = (seg(C_SBQ, C_SBK) * (LOG2E * SB_HEAD_DIM ** -0.5)).astype(bf16)
    sbk_ref[0] = seg(C_SBK, C_SBV).astype(bf16)
    sbv_ref[0] = seg(C_SBV, C_CQ).astype(bf16)

    cq = _rms_norm_rows(seg(C_CQ, C_CKV), qn_ref[...])
    q = _dot(cq.astype(bf16), wuq_ref[...])
    mla_scale = LOG2E * (MLA_NOPE + MLA_ROPE) ** -0.5
    for hd in range(MLA_HEADS):
        lo = hd * MLA_QK_PAD
        blk = _rope_tile(q[:, lo:lo + MLA_QK_PAD], mtab_ref, MLA_ROPE // 2)
        mq_ref[0, :, lo:lo + MLA_QK_PAD] = (blk * mla_scale).astype(bf16)
    ckv = _rms_norm_rows(seg(C_CKV, C_KR), kvn_ref[...]).astype(bf16)
    kn = _dot(ckv, wuk_ref[...])
    kr = _rope_tile(seg(C_KR, C_RQ), mtab_ref, MLA_ROPE // 2)
    for hd in range(MLA_HEADS):
        lo = hd * MLA_QK_PAD
        mk_ref[0, :, lo:lo + MLA_QK_PAD] = (kn[:, lo:lo + MLA_QK_PAD] + kr).astype(bf16)
    mv = _dot(ckv, wuv_ref[...])
    lane = lax.broadcasted_iota(jnp.int32, (tm, LANES), 1)
    for hd in range(MLA_HEADS):
        pair_tile = mv[:, (hd // 2) * LANES:(hd // 2 + 1) * LANES]
        own = (lane < MLA_V) if hd % 2 == 0 else (lane >= MLA_V)
        mv_ref[0, :, hd * LANES:(hd + 1) * LANES] = jnp.where(own, pair_tile, 1.0).astype(bf16)

    row = lax.broadcasted_iota(jnp.int32, (tm, LANES), 0) + pl.program_id(1) * tm
    kscale = jnp.where(row >= N_PAD, RET_QK ** -0.5, 0.0).astype(jnp.float32)
    rq = seg(C_RQ, C_RK)
    rk = seg(C_RK, C_RV)
    for grp in range(D_RQK // LANES):
        lo = grp * LANES
        rq_ref[0, :, lo:lo + LANES] = _rope_tile(rq[:, lo:lo + LANES], rtab_ref, RET_QK // 2).astype(bf16)
        rk_ref[0, :, lo:lo + LANES] = (_rope_tile(rk[:, lo:lo + LANES], rtab_ref, RET_QK // 2)
                                       * kscale).astype(bf16)
    rv_ref[0] = seg(C_RV, C_RG).astype(bf16)
    rg_ref[0] = seg(C_RG, N_IN_PAD).astype(bf16)


def _projections(h, win, qn, kvn, wuq, wuk, wuv, mtab, rtab):
    bsz, length, d = h.shape
    tm = _row_tile(length, 704)
    const2 = lambda i, j: (0, 0)
    tok = lambda w: pl.BlockSpec((1, tm, w), lambda i, j: (i, j, 0))
    widths = (D_SB, D_SB, D_SB, MLA_HEADS * MLA_QK_PAD, MLA_HEADS * MLA_QK_PAD, MLA_HEADS * LANES,
              D_RQK, D_RQK, D_RET, D_RET)
    return pl.pallas_call(
        functools.partial(_proj_kernel, tm=tm),
        grid=(bsz, length // tm),
        in_specs=[tok(d),
                  pl.BlockSpec(win.shape, const2),
                  pl.BlockSpec(qn.shape, const2),
                  pl.BlockSpec(kvn.shape, const2),
                  pl.BlockSpec(wuq.shape, const2),
                  pl.BlockSpec(wuk.shape, const2),
                  pl.BlockSpec(wuv.shape, const2),
                  pl.BlockSpec((3, tm, LANES), lambda i, j: (0, j, 0)),
                  pl.BlockSpec((3, tm, LANES), lambda i, j: (0, j, 0))],
        out_specs=[tok(w) for w in widths],
        out_shape=[jax.ShapeDtypeStruct((bsz, length, w), jnp.bfloat16) for w in widths],
        compiler_params=_cparams("parallel", "parallel"),
        name="projections",
    )(h, win, qn, kvn, wuq, wuk, wuv, mtab, rtab)


SB_STACK = 2 * BLOCK
SB_LIGHT_ROWS = 32


def _sb_step(qstack, kbs, vbs, r_prev, mask, uo):
    bf16 = jnp.bfloat16
    stack = qstack[0].shape[0]
    half = stack // 2
    head0 = lax.broadcasted_iota(jnp.int32, (half, LANES), 1) < SB_HEAD_DIM
    z = jnp.concatenate([_dot_nt(q, kb) for q, kb in zip(qstack, kbs)], axis=0)
    sp = jnp.log(1.0 + jnp.exp2(-jnp.abs(z))) * LOG2E
    lb = jnp.minimum(z, 0.0) - sp
    lk = lb - z
    if mask is not None:
        lk = jnp.where(mask, lk, 0.0)
    hi = lk.astype(bf16)
    mid = (lk - hi.astype(jnp.float32)).astype(bf16)
    ex = _dot(jnp.concatenate([hi, mid], axis=1), uo)
    arg = lb + ex[:, :LANES]
    if r_prev is not None:
        arg = arg + r_prev
    w = jnp.exp2(arg)
    if mask is not None:
        w = jnp.where(mask, w, 0.0)
    wb = w.astype(bf16)
    r_new = ex[:, LANES:] if r_prev is None else r_prev + ex[:, LANES:]
    outs = []
    for g, vb in enumerate(vbs):
        pv = _dot(wb[g * stack:(g + 1) * stack], vb)
        outs.append(jnp.where(head0, pv[:half], pv[half:]))
    return outs, r_new


def _sb_kernel(q_ref, k_ref, v_ref, uo_ref, o_ref, r_ref, rmax_ref, *, nblk, grp):
    uo = uo_ref[...]
    head0 = lax.broadcasted_iota(jnp.int32, (BLOCK, LANES), 1) < SB_HEAD_DIM
    zero = jnp.zeros((BLOCK, LANES), jnp.bfloat16)
    rows_all = grp * SB_STACK
    col = lax.broadcasted_iota(jnp.int32, (rows_all, LANES), 1)
    row = lax.broadcasted_iota(jnp.int32, (rows_all, LANES), 0)
    stream = row // SB_STACK
    light = SB_LIGHT_ROWS
    starts = [g * SB_STACK + h * BLOCK for g in range(grp) for h in range(2)]
    lead = [pl.ds(s, light) for s in starts]
    lead_static = [slice(s, s + light) for s in starts]
    rest_static = [slice(s + light, s + BLOCK) for s in starts]

    def per_stream(values):
        tile = jnp.full((rows_all, LANES), values[-1], jnp.int32)
        for g in range(grp - 2, -1, -1):
            tile = jnp.where(stream == g, values[g], tile)
        return tile

    def group(gi, carry):
        blocks = [gi * grp + g for g in range(grp)]
        rows = [pl.ds(pl.multiple_of(i * BLOCK, BLOCK), BLOCK) for i in blocks]
        qstack, qlight = [], []
        for g in range(grp):
            qb = q_ref[0, rows[g], :]
            q0, q1 = jnp.where(head0, qb, zero), jnp.where(head0, zero, qb)
            qstack.append(jnp.concatenate([q0, q1], axis=0))
            qlight.append(jnp.concatenate([q0[:light], q1[:light]], axis=0))

        first_valid = per_stream([jnp.where(b == 0, N_PAD, 0) for b in blocks])
        mask = (col < (row % BLOCK)) & (col >= first_valid)
        outs, r = _sb_step(qstack, [k_ref[0, rw, :] for rw in rows], [v_ref[0, rw, :] for rw in rows],
                           None, mask, uo)
        r_ref[...] = r
        for g in range(grp):
            o_ref[0, rows[g], :] = outs[g]

        def key_blocks(d):
            js = [b - d for b in blocks]
            krows = [pl.ds(pl.multiple_of(jnp.maximum(j, 0) * BLOCK, BLOCK), BLOCK) for j in js]
            return js, [k_ref[0, rw, :] for rw in krows], [v_ref[0, rw, :] for rw in krows]

        def full_step(d, with_mask):
            js, kbs, vbs = key_blocks(d)
            mask = None
            if with_mask:
                fv = per_stream([jnp.where(j < 0, BLOCK, jnp.where(j == 0, N_PAD, 0)) for j in js])
                mask = col >= fv
            outs, r_new = _sb_step(qstack, kbs, vbs, r_ref[...], mask, uo)
            r_ref[...] = r_new
            for g in range(grp):
                o_ref[0, rows[g], :] += outs[g]
            if not with_mask:
                rmax_ref[0] = jnp.max(functools.reduce(jnp.maximum, [r_new[sl] for sl in lead_static]))
                rmax_ref[1] = jnp.max(functools.reduce(jnp.maximum, [r_new[sl] for sl in rest_static]))

        def light_step(d):
            _, kbs, vbs = key_blocks(d)
            r_prev = jnp.concatenate([r_ref[sl, :] for sl in lead], axis=0)
            outs, r_new = _sb_step(qlight, kbs, vbs, r_prev, None, uo)
            for i, sl in enumerate(lead):
                r_ref[sl, :] = r_new[i * light:(i + 1) * light]
            for g in range(grp):
                lead_rows = pl.ds(pl.multiple_of(blocks[g] * BLOCK, BLOCK), light)
                o_ref[0, lead_rows, :] += outs[g]
            rmax_ref[0] = jnp.max(r_new)

        def unmasked(c):
            d, _, full = c

            @pl.when(full)
            def _():
                full_step(d, False)

            @pl.when(jnp.logical_not(full))
            def _():
                light_step(d)

            full = rmax_ref[1] > -SB_LOG2_CUTOFF
            return d + 1, full | (rmax_ref[0] > -SB_LOG2_CUTOFF), full

        c = lax.while_loop(lambda c: c[1] & (blocks[0] - c[0] >= 1), unmasked,
                           (jnp.int32(1), blocks[-1] >= 1, jnp.bool_(True)))

        def masked(c):
            d, _ = c
            full_step(d, True)
            need = jnp.float32(-jnp.inf)
            for g in range(grp):
                rmax = jnp.max(r_ref[g * SB_STACK:(g + 1) * SB_STACK, :])
                need = jnp.maximum(need, jnp.where(blocks[g] - d >= 1, rmax, -jnp.inf))
            return d + 1, need > -SB_LOG2_CUTOFF

        lax.while_loop(lambda c: c[1], masked, (c[0], c[1]))
        return carry

    lax.fori_loop(0, nblk // grp, group, 0)


def _stick_breaking(q, k, v, uo):
    bsz, length, _ = q.shape
    nblk = length // BLOCK
    grp = max(g for g in range(1, SB_GROUP_MAX + 1) if nblk % g == 0)
    pair = lambda: pl.BlockSpec((1, length, LANES), lambda b, p: (b, 0, p))
    return pl.pallas_call(
        functools.partial(_sb_kernel, nblk=nblk, grp=grp),
        grid=(bsz, D_SB // LANES),
        in_specs=[pair(), pair(), pair(), pl.BlockSpec(uo.shape, lambda b, p: (0, 0))],
        out_specs=pl.BlockSpec((1, length, LANES), lambda b, p: (b, 0, p)),
        out_shape=jax.ShapeDtypeStruct((bsz, length, D_SB), jnp.float32),
        scratch_shapes=[pltpu.VMEM((grp * SB_STACK, LANES), jnp.float32), pltpu.SMEM((2,), jnp.float32)],
        compiler_params=_cparams("parallel", "parallel"),
        name="stick_breaking",
    )(q, k, v, uo)


MLA_BIAS_DIAG, MLA_BIAS_DIAG_FIRST, MLA_BIAS_PAD, MLA_BIAS_NONE = range(4)
MLA_UNROLL_MAX = 6


def _mla_bias_table(tq):
    row = np.arange(tq)[:, None]
    col = np.arange(tq)[None, :]
    keep = np.stack([col <= row,
                     (col <= row) & ((col >= N_PAD) | (col == row)),
                     np.broadcast_to(col >= N_PAD, (tq, tq)),
                     np.ones((tq, tq), bool)])
    return jnp.asarray(np.where(keep, 0.0, NEG_BIG), jnp.float32)


def _mla_kernel(q_ref, k_ref, v_ref, bias_ref, o_ref, sa_ref, sb_ref, m_ref, acc_ref, *, tq, nq):
    bf16 = jnp.bfloat16
    lane = lax.broadcasted_iota(jnp.int32, (tq, LANES), 1)
    n_steps = nq * (nq + 1) // 2
    assert n_steps % 2 == 0, "the two logits buffers alternate, so steps are taken in pairs"
    unroll = max(u for u in range(2, MLA_UNROLL_MAX + 1, 2) if n_steps % u == 0)

    def rows(i):
        return pl.ds(pl.multiple_of(i * tq, tq), tq)

    def key_tile(qi, pos):
        return jnp.where(pos == 0, qi, pos - 1)

    def advance(qi, pos):
        last = pos == qi
        return jnp.where(last, jnp.minimum(qi + 1, nq - 1), qi), jnp.where(last, 0, pos + 1)

    def write_scores(buf, qi, pos):
        kt = key_tile(qi, pos)
        kind = jnp.where(pos == 0, jnp.where(qi == 0, MLA_BIAS_DIAG_FIRST, MLA_BIAS_DIAG),
                         jnp.where(kt == 0, MLA_BIAS_PAD, MLA_BIAS_NONE))
        for hd in range(2):
            cols = slice(hd * MLA_QK_PAD, (hd + 1) * MLA_QK_PAD)
            buf[hd] = _dot_nt(q_ref[0, rows(qi), cols], k_ref[0, rows(kt), cols]) + bias_ref[kind]

    def update(buf, qi, pos):
        kt = key_tile(qi, pos)
        for hd in range(2):
            m = jnp.where(pos == 0, NEG_BIG, m_ref[hd])
            s = buf[hd]
            m_new = jnp.maximum(m, jnp.max(s, axis=-1, keepdims=True))
            p = jnp.concatenate([jnp.exp2(s[:, c:c + LANES] - m_new) for c in range(0, tq, LANES)], axis=1)
            pv = _dot(p.astype(bf16), v_ref[0, rows(kt), hd * LANES:(hd + 1) * LANES])
            acc_ref[hd, rows(qi)] = jnp.exp2(m - m_new) * acc_ref[hd, rows(qi)] + pv
            m_ref[hd] = m_new

    acc_ref[...] = jnp.zeros(acc_ref.shape, jnp.float32)
    m_ref[...] = jnp.full(m_ref.shape, NEG_BIG, jnp.float32)
    write_scores(sa_ref, 0, 0)

    def steps(i, cur):
        for _ in range(unroll // 2):
            nxt = advance(*cur)
            write_scores(sb_ref, *nxt)
            update(sa_ref, *cur)
            cur = advance(*nxt)
            write_scores(sa_ref, *cur)
            update(sb_ref, *nxt)
        return cur

    lax.fori_loop(0, n_steps // unroll, steps, (jnp.int32(0), jnp.int32(0)))

    def normalize(qi, carry):
        acc0, acc1 = acc_ref[0, rows(qi)], acc_ref[1, rows(qi)]
        n0 = acc0 / pltpu.roll(acc0, MLA_V, 1)
        n1 = acc1 / pltpu.roll(acc1, MLA_V, 1)
        o_ref[0, rows(qi), :] = jnp.where(lane < MLA_V, n0, n1).astype(o_ref.dtype)
        return carry

    lax.fori_loop(0, nq, normalize, 0)


def _mla_attention(q, k, v):
    bsz, length, _ = q.shape
    tq = _row_tile(length, 384)
    bias = _mla_bias_table(tq)
    seq = lambda w: pl.BlockSpec((1, length, w), lambda b, p: (b, 0, p))
    return pl.pallas_call(
        functools.partial(_mla_kernel, tq=tq, nq=length // tq),
        grid=(bsz, MLA_HEADS // 2),
        in_specs=[seq(2 * MLA_QK_PAD), seq(2 * MLA_QK_PAD), seq(2 * LANES),
                  pl.BlockSpec(bias.shape, lambda b, p: (0, 0, 0))],
        out_specs=seq(LANES),
        out_shape=jax.ShapeDtypeStruct((bsz, length, D_MLA), jnp.bfloat16),
        scratch_shapes=[pltpu.VMEM((2, tq, tq), jnp.float32), pltpu.VMEM((2, tq, tq), jnp.float32),
                        pltpu.VMEM((2, tq, LANES), jnp.float32), pltpu.VMEM((2, length, LANES), jnp.float32)],
        compiler_params=_cparams("parallel", "parallel"),
        name="mla_attention",
    )(q, k, v, bias)


RET_UNROLL_MAX = 11


def _ret_kernel(q_ref, k_ref, v_ref, g_ref, dec_ref, o_ref, *, nblk):
    bf16 = jnp.bfloat16
    f32 = jnp.float32
    lane = lax.broadcasted_iota(jnp.int32, (BLOCK, LANES), 1)
    lane2 = lax.broadcasted_iota(jnp.int32, (BLOCK, 2 * RET_V), 1)
    feat = lax.broadcasted_iota(jnp.int32, (BLOCK, 2 * RET_V), 0)
    zero = jnp.zeros((BLOCK, LANES), bf16)
    zero2 = jnp.zeros((BLOCK, 2 * RET_V), bf16)
    d_in = jnp.concatenate([dec_ref[0, 0], dec_ref[1, 0]], axis=0)
    q_decay = jnp.concatenate([dec_ref[0, 1], dec_ref[1, 1]], axis=1)
    k_decay = jnp.where(lane < RET_QK, dec_ref[0, 2], dec_ref[1, 2])
    c_decay = jnp.concatenate([dec_ref[0, 3], dec_ref[1, 3]], axis=1)
    same_head = ((feat < RET_QK) == (lane2 < RET_V)).astype(f32)

    def chunks(ci, state):
        for u in range(unroll):
            state = chunk(ci * unroll + u, state)
        return state

    def chunk(c, state):
        rows = pl.ds(pl.multiple_of(c * BLOCK, BLOCK), BLOCK)
        qb = q_ref[0, rows, :]
        kb = k_ref[0, rows, :]
        vt = v_ref[0, rows, :]
        qstack = jnp.concatenate([jnp.where(lane < RET_QK, qb, zero), jnp.where(lane < RET_QK, zero, qb)], axis=0)
        inner = _dot_nt(qstack, kb) * d_in
        inner_cat = jnp.concatenate([inner[:BLOCK], inner[BLOCK:]], axis=1).astype(bf16)
        v_blocks = jnp.concatenate([jnp.where(lane2 < RET_V, vt, zero2), jnp.where(lane2 < RET_V, zero2, vt)], axis=0)
        y = _dot(inner_cat, v_blocks) + _dot(qb, state.astype(bf16)) * q_decay
        kd = (kb.astype(f32) * k_decay).astype(bf16)
        state = state * c_decay + _dot_tn(kd, vt) * same_head
        for hd in range(2):
            cols = slice(hd * RET_V, (hd + 1) * RET_V)
            yh = y[:, cols]
            mu = jnp.mean(yh, axis=-1, keepdims=True)
            yc = yh - mu
            var = jnp.mean(yc * yc, axis=-1, keepdims=True)
            yn = yc * lax.rsqrt(var + LN_EPS)
            gate = g_ref[0, rows, cols].astype(f32)
            o_ref[0, rows, cols] = (gate * jax.nn.sigmoid(gate) * yn).astype(o_ref.dtype)
        return state

    unroll = max(u for u in range(1, RET_UNROLL_MAX + 1) if nblk % u == 0)
    lax.fori_loop(0, nblk // unroll, chunks, jnp.zeros((BLOCK, 2 * RET_V), f32))


def _retention(q, k, v, g, dec):
    bsz, length, _ = v.shape
    qk = lambda: pl.BlockSpec((1, length, LANES), lambda b, p: (b, 0, p))
    vg = lambda: pl.BlockSpec((1, length, 2 * RET_V), lambda b, p: (b, 0, p))
    return pl.pallas_call(
        functools.partial(_ret_kernel, nblk=length // BLOCK),
        grid=(bsz, RET_HEADS // 2),
        in_specs=[qk(), qk(), vg(), vg(),
                  pl.BlockSpec((2, 4, BLOCK, LANES), lambda b, p: (p, 0, 0, 0))],
        out_specs=pl.BlockSpec((1, length, 2 * RET_V), lambda b, p: (b, 0, p)),
        out_shape=jax.ShapeDtypeStruct((bsz, length, D_RET), jnp.bfloat16),
        compiler_params=_cparams("parallel", "parallel"),
        name="retention",
    )(q, k, v, g, dec)


def _out_kernel(a_ref, b_ref, c_ref, h_ref, w_ref, g_ref, beta_ref, o_ref):
    bf16 = jnp.bfloat16
    mix = (_dot(a_ref[0].astype(bf16), w_ref[0:D_SB, :])
           + _dot(b_ref[0], w_ref[D_SB:D_SB + D_MLA, :])
           + _dot(c_ref[0], w_ref[D_SB + D_MLA:D_MIX, :]))
    o_ref[0] = _layer_norm_rows(DN_ALPHA * h_ref[0] + mix, g_ref[...], beta_ref[...])


def _out_projection(a, b, c, h, w, g, beta):
    bsz, length, d = h.shape
    tm = _row_tile(length, 704)
    tok = lambda wd: pl.BlockSpec((1, tm, wd), lambda i, j: (i, j, 0))
    const2 = lambda i, j: (0, 0)
    return pl.pallas_call(
        _out_kernel,
        grid=(bsz, length // tm),
        in_specs=[tok(D_SB), tok(D_MLA), tok(D_RET), tok(d),
                  pl.BlockSpec(w.shape, const2), pl.BlockSpec(g.shape, const2),
                  pl.BlockSpec(beta.shape, const2)],
        out_specs=tok(d),
        out_shape=jax.ShapeDtypeStruct(h.shape, jnp.float32),
        compiler_params=_cparams("parallel", "parallel"),
        name="out_projection",
    )(a, b, c, h, w, g, beta)


FF_CHUNK = 1024


def _mlp_kernel(h_ref, w1_ref, w2_ref, g_ref, beta_ref, o_ref):
    bf16 = jnp.bfloat16
    h = h_ref[0]
    hb = h.astype(bf16)
    acc = DN_ALPHA * h
    for lo in range(0, D_FF, FF_CHUNK):
        u = jnp.maximum(_dot(hb, w1_ref[:, lo:lo + FF_CHUNK]), 0.0)
        acc = acc + _dot((u * u).astype(bf16), w2_ref[lo:lo + FF_CHUNK, :])
    o_ref[0] = _layer_norm_rows(acc, g_ref[...], beta_ref[...])


def _mlp(h, w1, w2, g, beta):
    bsz, length, d = h.shape
    tm = _row_tile(length, 704)
    tok = pl.BlockSpec((1, tm, d), lambda i, j: (i, j, 0))
    const2 = lambda i, j: (0, 0)
    return pl.pallas_call(
        _mlp_kernel,
        grid=(bsz, length // tm),
        in_specs=[tok, pl.BlockSpec(w1.shape, const2), pl.BlockSpec(w2.shape, const2),
                  pl.BlockSpec(g.shape, const2), pl.BlockSpec(beta.shape, const2)],
        out_specs=tok,
        out_shape=jax.ShapeDtypeStruct(h.shape, jnp.float32),
        compiler_params=_cparams("parallel", "parallel"),
        name="mlp",
    )(h, w1, w2, g, beta)


def _rope_tables(length):
    pos = (jnp.arange(length) - N_PAD).astype(jnp.float32)

    def angles(half):
        inv = ROPE_THETA ** (-jnp.arange(half, dtype=jnp.float32) / half)
        ang = pos[:, None] * inv[None, :]
        return jnp.cos(ang), jnp.sin(ang)

    cos, sin = angles(MLA_ROPE // 2)
    one = jnp.ones((length, MLA_NOPE), jnp.float32)
    z = lambda w: jnp.zeros((length, w), jnp.float32)
    mla = jnp.stack([jnp.concatenate([one, cos, cos, z(32)], 1),
                     jnp.concatenate([z(64), -sin, z(16), z(32)], 1),
                     jnp.concatenate([z(64), z(16), sin, z(32)], 1)])
    cos, sin = angles(RET_QK // 2)
    ret = jnp.stack([jnp.concatenate([cos, cos] * 2, 1),
                     jnp.concatenate([-sin, z(32)] * 2, 1),
                     jnp.concatenate([z(32), sin] * 2, 1)])
    return mla, ret


def _retention_decays():
    log_g = jnp.log(jnp.array(RET_GAMMA, jnp.float32))
    idx = jnp.arange(BLOCK, dtype=jnp.float32)
    diff = idx[:, None] - idx[None, :]
    d_in = jnp.where(diff[None] >= 0, jnp.exp(jnp.maximum(diff, 0.0)[None] * log_g[:, None, None]), 0.0)
    q_decay = jnp.exp((idx[None, :] + 1.0) * log_g[:, None])
    k_decay = jnp.exp((BLOCK - 1.0 - idx[None, :]) * log_g[:, None])
    c_decay = jnp.exp(BLOCK * log_g)
    rep = lambda a: jnp.broadcast_to(a[:, :, None], (RET_HEADS, BLOCK, LANES))
    full = jnp.broadcast_to(c_decay[:, None, None], (RET_HEADS, BLOCK, LANES))
    return jnp.stack([d_in, rep(q_decay), rep(k_decay), full], axis=1)


def _suffix_sum_matrix():
    i = np.arange(BLOCK)
    upper = (i[:, None] > i[None, :]).astype(np.float32)
    ones = np.ones((BLOCK, BLOCK), np.float32)
    half = np.concatenate([upper, ones], axis=1)
    return jnp.asarray(np.concatenate([half, half], axis=0), jnp.bfloat16)


def kernel(x, meta_tokens, ln_emb_g, ln_emb_b, w_in, mla_q_norm, mla_kv_norm, w_uq, w_ukv, w_out,
           ln1_g, ln1_b, w_ff1, w_ff2, ln2_g, ln2_b):
    bsz, seq, d = x.shape
    assert d == D_MODEL
    bf16 = jnp.bfloat16
    depth = w_in.shape[0]

    meta = jnp.broadcast_to(meta_tokens[None].astype(x.dtype), (bsz, N_META, d))
    h = jnp.concatenate([jnp.zeros((bsz, N_PAD, d), x.dtype), meta, x], axis=1)
    length = h.shape[1]
    assert length % BLOCK == 0

    zc = lambda n: jnp.zeros((depth, d, n), w_in.dtype)
    k_r_lo = 3 * D_SB + MLA_Q_LORA + MLA_KV_LORA
    win = jnp.concatenate([w_in[..., :k_r_lo], zc(MLA_NOPE), w_in[..., k_r_lo:k_r_lo + MLA_ROPE],
                           zc(LANES - MLA_NOPE - MLA_ROPE), w_in[..., k_r_lo + MLA_ROPE:]], axis=-1).astype(bf16)
    assert win.shape[-1] == N_IN_PAD
    wuq = w_uq.reshape(depth, MLA_Q_LORA, MLA_HEADS, MLA_NOPE + MLA_ROPE)
    wuq = jnp.pad(wuq, ((0, 0), (0, 0), (0, 0), (0, MLA_QK_PAD - MLA_NOPE - MLA_ROPE)))
    wuq = wuq.reshape(depth, MLA_Q_LORA, MLA_HEADS * MLA_QK_PAD).astype(bf16)
    wukv = w_ukv.reshape(depth, MLA_KV_LORA, MLA_HEADS, MLA_NOPE + MLA_V)
    wuk = jnp.pad(wukv[..., :MLA_NOPE], ((0, 0), (0, 0), (0, 0), (0, MLA_QK_PAD - MLA_NOPE)))
    wuk = wuk.reshape(depth, MLA_KV_LORA, MLA_HEADS * MLA_QK_PAD).astype(bf16)
    wuv = wukv[..., MLA_NOPE:].reshape(depth, MLA_KV_LORA, D_MLA).astype(bf16)
    wout = w_out.astype(bf16)
    w1 = w_ff1.astype(bf16)
    w2 = w_ff2.astype(bf16)

    mtab, rtab = _rope_tables(length)
    dec = _retention_decays()
    uo = _suffix_sum_matrix()
    row2 = lambda a: a.reshape(1, -1)

    h = _embed_layer_norm(h, ln_emb_g, ln_emb_b)
    for l in range(depth):
        sbq, sbk, sbv, mq, mk, mv, rq, rk, rv, rg = _projections(
            h, win[l], row2(mla_q_norm[l]), row2(mla_kv_norm[l]), wuq[l], wuk[l], wuv[l], mtab, rtab)
        out_a = _stick_breaking(sbq, sbk, sbv, uo)
        out_b = _mla_attention(mq, mk, mv)
        out_c = _retention(rq, rk, rv, rg, dec)
        h = _out_projection(out_a, out_b, out_c, h, wout[l], row2(ln1_g[l]), row2(ln1_b[l]))
        h = _mlp(h, w1[l], w2[l], row2(ln2_g[l]), row2(ln2_b[l]))
    return h[:, N_PAD + N_META:]
```

```python
import functools

import numpy as np
import jax
import jax.numpy as jnp
from jax import lax
from jax.experimental import pallas as pl
from jax.experimental.pallas import tpu as pltpu

D_MODEL = 1024
DEPTH = 4
N_META = 16
BLOCK = 128
N_PAD = (-N_META) % BLOCK

SB_HEADS, SB_HEAD_DIM = 8, 64
MLA_HEADS, MLA_NOPE, MLA_ROPE, MLA_V = 8, 64, 32, 64
MLA_Q_LORA, MLA_KV_LORA = 384, 256
RET_HEADS, RET_QK, RET_V = 4, 64, 128

D_SB = SB_HEADS * SB_HEAD_DIM
D_MLA = MLA_HEADS * MLA_V
D_RET = RET_HEADS * RET_V
D_MIX = D_SB + D_MLA + D_RET
D_FF = 4 * D_MODEL
D_RQK = RET_HEADS * RET_QK

ROPE_THETA = 10000.0
LN_EPS = 1e-5
DN_ALPHA = (2 * DEPTH) ** 0.25
RET_GAMMA = tuple(1.0 - 2.0 ** (-5 - h) for h in range(RET_HEADS))

LANES = 128
MLA_QK_PAD = LANES

C_SBQ, C_SBK, C_SBV = 0, D_SB, 2 * D_SB
C_CQ = 3 * D_SB
C_CKV = C_CQ + MLA_Q_LORA
C_KR = C_CKV + MLA_KV_LORA
C_RQ = C_KR + LANES
C_RK = C_RQ + D_RQK
C_RV = C_RK + D_RQK
C_RG = C_RV + D_RET
N_IN_PAD = C_RG + D_RET

VMEM_LIMIT_BYTES = 56 * 1024 * 1024

LOG2E = 1.4426950408889634
SB_LOG_CUTOFF = 105.0
SB_LOG2_CUTOFF = SB_LOG_CUTOFF * LOG2E
SB_GROUP_MAX = 11
NEG_BIG = -1e30


def _row_tile(length, cap):
    best = None
    for t in range(16, min(length, cap) + 1, 16):
        if length % t == 0:
            best = t
    assert best is not None, length
    return best


def _cparams(*sem):
    return pltpu.CompilerParams(dimension_semantics=sem, vmem_limit_bytes=VMEM_LIMIT_BYTES)


def _layer_norm_rows(x, g, b):
    mu = jnp.mean(x, axis=-1, keepdims=True)
    xc = x - mu
    var = jnp.mean(xc * xc, axis=-1, keepdims=True)
    return xc * lax.rsqrt(var + LN_EPS) * g + b


def _rms_norm_rows(x, g):
    return x * lax.rsqrt(jnp.mean(x * x, axis=-1, keepdims=True) + LN_EPS) * g


def _dot(a, b):
    return jnp.dot(a, b, preferred_element_type=jnp.float32)


def _dot_nt(a, b):
    return lax.dot_general(a, b, (((1,), (1,)), ((), ())), preferred_element_type=jnp.float32)


def _dot_tn(a, b):
    return lax.dot_general(a, b, (((0,), (0,)), ((), ())), preferred_element_type=jnp.float32)


def _ln_kernel(x_ref, g_ref, b_ref, o_ref):
    o_ref[...] = _layer_norm_rows(x_ref[...], g_ref[...], b_ref[...])


def _embed_layer_norm(h, g, b):
    bsz, length, d = h.shape
    tm = _row_tile(length, 1408)
    return pl.pallas_call(
        _ln_kernel,
        grid=(bsz, length // tm),
        in_specs=[pl.BlockSpec((1, tm, d), lambda i, j: (i, j, 0)),
                  pl.BlockSpec((1, 1, d), lambda i, j: (0, 0, 0)),
                  pl.BlockSpec((1, 1, d), lambda i, j: (0, 0, 0))],
        out_specs=pl.BlockSpec((1, tm, d), lambda i, j: (i, j, 0)),
        out_shape=jax.ShapeDtypeStruct(h.shape, jnp.float32),
        compiler_params=_cparams("parallel", "parallel"),
        name="embed_ln",
    )(h, g.reshape(1, 1, d), b.reshape(1, 1, d))


def _rope_tile(x, tab_ref, half):
    c, s1, s2 = tab_ref[0], tab_ref[1], tab_ref[2]
    return x * c + pltpu.roll(x, LANES - half, 1) * s1 + pltpu.roll(x, half, 1) * s2


def _proj_kernel(h_ref, win_ref, qn_ref, kvn_ref, wuq_ref, wuk_ref, wuv_ref, mtab_ref, rtab_ref,
                 sbq_ref, sbk_ref, sbv_ref, mq_ref, mk_ref, mv_ref, rq_ref, rk_ref, rv_ref, rg_ref,
                 *, tm):
    bf16 = jnp.bfloat16
    hb = h_ref[0].astype(bf16)

    groups = {}

    def seg(lo, hi):
        for g_lo, g_hi in ((C_SBQ, C_CQ), (C_CQ, C_RQ), (C_RQ, N_IN_PAD)):
            if g_lo <= lo and hi <= g_hi:
                if g_lo not in groups:
                    groups[g_lo] = _dot(hb, win_ref[:, g_lo:g_hi])
                return groups[g_lo][:, lo - g_lo:hi - g_lo]
        raise ValueError((lo, hi))

    sbq_ref[0] = (seg(C_SBQ, C_SBK) * (LOG2E * SB_HEAD_DIM ** -0.5)).astype(bf16)
    sbk_ref[0] = seg(C_SBK, C_SBV).astype(bf16)
    sbv_ref[0] = seg(C_SBV, C_CQ).astype(bf16)

    cq = _rms_norm_rows(seg(C_CQ, C_CKV), qn_ref[...])
    q = _dot(cq.astype(bf16), wuq_ref[...])
    mla_scale = LOG2E * (MLA_NOPE + MLA_ROPE) ** -0.5
    for hd in range(MLA_HEADS):
        lo = hd * MLA_QK_PAD
        blk = _rope_tile(q[:, lo:lo + MLA_QK_PAD], mtab_ref, MLA_ROPE // 2)
        mq_ref[0, :, lo:lo + MLA_QK_PAD] = (blk * mla_scale).astype(bf16)
    ckv = _rms_norm_rows(seg(C_CKV, C_KR), kvn_ref[...]).astype(bf16)
    kn = _dot(ckv, wuk_ref[...])
    kr = _rope_tile(seg(C_KR, C_RQ), mtab_ref, MLA_ROPE // 2)
    for hd in range(MLA_HEADS):
        lo = hd * MLA_QK_PAD
        mk_ref[0, :, lo:lo + MLA_QK_PAD] = (kn[:, lo:lo + MLA_QK_PAD] + kr).astype(bf16)
    mv = _dot(ckv, wuv_ref[...])
    lane = lax.broadcasted_iota(jnp.int32, (tm, LANES), 1)
    for hd in range(MLA_HEADS):
        pair_tile = mv[:, (hd // 2) * LANES:(hd // 2 + 1) * LANES]
        own = (lane < MLA_V) if hd % 2 == 0 else (lane >= MLA_V)
        mv_ref[0, :, hd * LANES:(hd + 1) * LANES] = jnp.where(own, pair_tile, 1.0).astype(bf16)

    row = lax.broadcasted_iota(jnp.int32, (tm, LANES), 0) + pl.program_id(1) * tm
    kscale = jnp.where(row >= N_PAD, RET_QK ** -0.5, 0.0).astype(jnp.float32)
    rq = seg(C_RQ, C_RK)
    rk = seg(C_RK, C_RV)
    for grp in range(D_RQK // LANES):
        lo = grp * LANES
        rq_ref[0, :, lo:lo + LANES] = _rope_tile(rq[:, lo:lo + LANES], rtab_ref, RET_QK // 2).astype(bf16)
        rk_ref[0, :, lo:lo + LANES] = (_rope_tile(rk[:, lo:lo + LANES], rtab_ref, RET_QK // 2)
                                       * kscale).astype(bf16)
    rv_ref[0] = seg(C_RV, C_RG).astype(bf16)
    rg_ref[0] = seg(C_RG, N_IN_PAD).astype(bf16)


def _projections(h, win, qn, kvn, wuq, wuk, wuv, mtab, rtab):
    bsz, length, d = h.shape
    tm = _row_tile(length, 704)
    const2 = lambda i, j: (0, 0)
    tok = lambda w: pl.BlockSpec((1, tm, w), lambda i, j: (i, j, 0))
    widths = (D_SB, D_SB, D_SB, MLA_HEADS * MLA_QK_PAD, MLA_HEADS * MLA_QK_PAD, MLA_HEADS * LANES,
              D_RQK, D_RQK, D_RET, D_RET)
    return pl.pallas_call(
        functools.partial(_proj_kernel, tm=tm),
        grid=(bsz, length // tm),
        in_specs=[tok(d),
                  pl.BlockSpec(win.shape, const2),
                  pl.BlockSpec(qn.shape, const2),
                  pl.BlockSpec(kvn.shape, const2),
                  pl.BlockSpec(wuq.shape, const2),
                  pl.BlockSpec(wuk.shape, const2),
                  pl.BlockSpec(wuv.shape, const2),
                  pl.BlockSpec((3, tm, LANES), lambda i, j: (0, j, 0)),
                  pl.BlockSpec((3, tm, LANES), lambda i, j: (0, j, 0))],
        out_specs=[tok(w) for w in widths],
        out_shape=[jax.ShapeDtypeStruct((bsz, length, w), jnp.bfloat16) for w in widths],
        compiler_params=_cparams("parallel", "parallel"),
        name="projections",
    )(h, win, qn, kvn, wuq, wuk, wuv, mtab, rtab)


SB_STACK = 2 * BLOCK
SB_LIGHT_ROWS = 32


def _sb_step(qstack, kbs, vbs, r_prev, mask, uo):
    bf16 = jnp.bfloat16
    stack = qstack[0].shape[0]
    half = stack // 2
    head0 = lax.broadcasted_iota(jnp.int32, (half, LANES), 1) < SB_HEAD_DIM
    z = jnp.concatenate([_dot_nt(q, kb) for q, kb in zip(qstack, kbs)], axis=0)
    sp = jnp.log(1.0 + jnp.exp2(-jnp.abs(z))) * LOG2E
    lb = jnp.minimum(z, 0.0) - sp
    lk = lb - z
    if mask is not None:
        lk = jnp.where(mask, lk, 0.0)
    hi = lk.astype(bf16)
    mid = (lk - hi.astype(jnp.float32)).astype(bf16)
    ex = _dot(jnp.concatenate([hi, mid], axis=1), uo)
    arg = lb + ex[:, :LANES]
    if r_prev is not None:
        arg = arg + r_prev
    w = jnp.exp2(arg)
    if mask is not None:
        w = jnp.where(mask, w, 0.0)
    wb = w.astype(bf16)
    r_new = ex[:, LANES:] if r_prev is None else r_prev + ex[:, LANES:]
    outs = []
    for g, vb in enumerate(vbs):
        pv = _dot(wb[g * stack:(g + 1) * stack], vb)
        outs.append(jnp.where(head0, pv[:half], pv[half:]))
    return outs, r_new


def _sb_kernel(q_ref, k_ref, v_ref, uo_ref, o_ref, r_ref, rmax_ref, *, nblk, grp):
    uo = uo_ref[...]
    head0 = lax.broadcasted_iota(jnp.int32, (BLOCK, LANES), 1) < SB_HEAD_DIM
    zero = jnp.zeros((BLOCK, LANES), jnp.bfloat16)
    rows_all = grp * SB_STACK
    col = lax.broadcasted_iota(jnp.int32, (rows_all, LANES), 1)
    row = lax.broadcasted_iota(jnp.int32, (rows_all, LANES), 0)
    light = SB_LIGHT_ROWS
    starts = [g * SB_STACK + h * BLOCK for g in range(grp) for h in range(2)]
    lead = [pl.ds(s, light) for s in starts]
    lead_static = [slice(s, s + light) for s in starts]
    rest_static = [slice(s + light, s + BLOCK) for s in starts]

    def per_stream(values, stack=SB_STACK):
        stream_of_row = lax.broadcasted_iota(jnp.int32, (grp * stack, LANES), 0) // stack
        tile = jnp.full((grp * stack, LANES), values[-1], jnp.int32)
        for g in range(grp - 2, -1, -1):
            tile = jnp.where(stream_of_row == g, values[g], tile)
        return tile

    def group(gi, carry):
        blocks = [gi * grp + g for g in range(grp)]
        rows = [pl.ds(pl.multiple_of(i * BLOCK, BLOCK), BLOCK) for i in blocks]
        qstack, qlight = [], []
        for g in range(grp):
            qb = q_ref[0, rows[g], :]
            q0, q1 = jnp.where(head0, qb, zero), jnp.where(head0, zero, qb)
            qstack.append(jnp.concatenate([q0, q1], axis=0))
            qlight.append(jnp.concatenate([q0[:light], q1[:light]], axis=0))

        first_valid = per_stream([jnp.where(b == 0, N_PAD, 0) for b in blocks])
        mask = (col < (row % BLOCK)) & (col >= first_valid)
        outs, r = _sb_step(qstack, [k_ref[0, rw, :] for rw in rows], [v_ref[0, rw, :] for rw in rows],
                           None, mask, uo)
        r_ref[...] = r
        for g in range(grp):
            o_ref[0, rows[g], :] = outs[g]

        def key_blocks(d):
            js = [b - d for b in blocks]
            krows = [pl.ds(pl.multiple_of(jnp.maximum(j, 0) * BLOCK, BLOCK), BLOCK) for j in js]
            return js, [k_ref[0, rw, :] for rw in krows], [v_ref[0, rw, :] for rw in krows]

        def masks(js, stack):
            fv = per_stream([jnp.where(j < 0, BLOCK, jnp.where(j == 0, N_PAD, 0)) for j in js], stack)
            keep = lax.broadcasted_iota(jnp.int32, (grp * stack, LANES), 1) >= fv
            more = per_stream([(j >= 1).astype(jnp.int32) for j in js], stack) > 0
            return keep, more

        def full_step(d, with_mask):
            js, kbs, vbs = key_blocks(d)
            keep, more = masks(js, SB_STACK) if with_mask else (None, None)
            outs, r_new = _sb_step(qstack, kbs, vbs, r_ref[...], keep, uo)
            r_ref[...] = r_new
            for g in range(grp):
                o_ref[0, rows[g], :] += outs[g]
            if with_mask:
                r_new = jnp.where(more, r_new, -jnp.inf)
            rmax_ref[0] = jnp.max(functools.reduce(jnp.maximum, [r_new[sl] for sl in lead_static]))
            rmax_ref[1] = jnp.max(functools.reduce(jnp.maximum, [r_new[sl] for sl in rest_static]))

        def light_step(d, with_mask):
            js, kbs, vbs = key_blocks(d)
            keep, more = masks(js, 2 * light) if with_mask else (None, None)
            r_prev = jnp.concatenate([r_ref[sl, :] for sl in lead], axis=0)
            outs, r_new = _sb_step(qlight, kbs, vbs, r_prev, keep, uo)
            for i, sl in enumerate(lead):
                r_ref[sl, :] = r_new[i * light:(i + 1) * light]
            for g in range(grp):
                lead_rows = pl.ds(pl.multiple_of(blocks[g] * BLOCK, BLOCK), light)
                o_ref[0, lead_rows, :] += outs[g]
            if with_mask:
                r_new = jnp.where(more, r_new, -jnp.inf)
            rmax_ref[0] = jnp.max(r_new)

        def step(c):
            d, _, full = c
            plain = blocks[0] - d >= 1
            for is_full, is_plain in ((True, True), (False, True), (True, False), (False, False)):
                @pl.when((full == is_full) & (plain == is_plain))
                def _():
                    (full_step if is_full else light_step)(d, not is_plain)

            full = rmax_ref[1] > -SB_LOG2_CUTOFF
            return d + 1, full | (rmax_ref[0] > -SB_LOG2_CUTOFF), full

        lax.while_loop(lambda c: c[1], step, (jnp.int32(1), blocks[-1] >= 1, jnp.bool_(True)))
        return carry

    lax.fori_loop(0, nblk // grp, group, 0)


def _stick_breaking(q, k, v, uo):
    bsz, length, _ = q.shape
    nblk = length // BLOCK
    grp = max(g for g in range(1, SB_GROUP_MAX + 1) if nblk % g == 0)
    pair = lambda: pl.BlockSpec((1, length, LANES), lambda b, p: (b, 0, p))
    return pl.pallas_call(
        functools.partial(_sb_kernel, nblk=nblk, grp=grp),
        grid=(bsz, D_SB // LANES),
        in_specs=[pair(), pair(), pair(), pl.BlockSpec(uo.shape, lambda b, p: (0, 0))],
        out_specs=pl.BlockSpec((1, length, LANES), lambda b, p: (b, 0, p)),
        out_shape=jax.ShapeDtypeStruct((bsz, length, D_SB), jnp.float32),
        scratch_shapes=[pltpu.VMEM((grp * SB_STACK, LANES), jnp.float32), pltpu.SMEM((2,), jnp.float32)],
        compiler_params=_cparams("parallel", "parallel"),
        name="stick_breaking",
    )(q, k, v, uo)


MLA_BIAS_DIAG, MLA_BIAS_DIAG_FIRST, MLA_BIAS_PAD, MLA_BIAS_NONE = range(4)
MLA_UNROLL_MAX = 22


def _mla_bias_table(tq):
    row = np.arange(tq)[:, None]
    col = np.arange(tq)[None, :]
    keep = np.stack([col <= row,
                     (col <= row) & ((col >= N_PAD) | (col == row)),
                     np.broadcast_to(col >= N_PAD, (tq, tq)),
                     np.ones((tq, tq), bool)])
    return jnp.asarray(np.where(keep, 0.0, NEG_BIG), jnp.float32)


def _mla_kernel(q_ref, k_ref, v_ref, bias_ref, o_ref, sa_ref, sb_ref, m_ref, acc_ref, *, tq, nq):
    bf16 = jnp.bfloat16
    lane = lax.broadcasted_iota(jnp.int32, (tq, LANES), 1)
    n_steps = nq * (nq + 1) // 2
    assert n_steps % 2 == 0, "the two logits buffers alternate, so steps are taken in pairs"
    unroll = max(u for u in range(2, MLA_UNROLL_MAX + 1, 2) if n_steps % u == 0)

    def rows(i):
        return pl.ds(pl.multiple_of(i * tq, tq), tq)

    def key_tile(qi, pos):
        return jnp.where(pos == 0, qi, pos - 1)

    def advance(qi, pos):
        last = pos == qi
        return jnp.where(last, jnp.minimum(qi + 1, nq - 1), qi), jnp.where(last, 0, pos + 1)

    def write_scores(buf, qi, pos):
        kt = key_tile(qi, pos)
        kind = jnp.where(pos == 0, jnp.where(qi == 0, MLA_BIAS_DIAG_FIRST, MLA_BIAS_DIAG),
                         jnp.where(kt == 0, MLA_BIAS_PAD, MLA_BIAS_NONE))
        for hd in range(2):
            cols = slice(hd * MLA_QK_PAD, (hd + 1) * MLA_QK_PAD)
            buf[hd] = _dot_nt(q_ref[0, rows(qi), cols], k_ref[0, rows(kt), cols]) + bias_ref[kind]

    def update(buf, qi, pos):
        kt = key_tile(qi, pos)
        for hd in range(2):
            m = jnp.where(pos == 0, NEG_BIG, m_ref[hd])
            s = buf[hd]
            m_new = jnp.maximum(m, jnp.max(s, axis=-1, keepdims=True))
            p = jnp.concatenate([jnp.exp2(s[:, c:c + LANES] - m_new) for c in range(0, tq, LANES)], axis=1)
            pv = _dot(p.astype(bf16), v_ref[0, rows(kt), hd * LANES:(hd + 1) * LANES])
            acc_ref[hd, rows(qi)] = jnp.exp2(m - m_new) * acc_ref[hd, rows(qi)] + pv
            m_ref[hd] = m_new

    acc_ref[...] = jnp.zeros(acc_ref.shape, jnp.float32)
    m_ref[...] = jnp.full(m_ref.shape, NEG_BIG, jnp.float32)
    write_scores(sa_ref, 0, 0)

    def steps(i, cur):
        for _ in range(unroll // 2):
            nxt = advance(*cur)
            write_scores(sb_ref, *nxt)
            update(sa_ref, *cur)
            cur = advance(*nxt)
            write_scores(sa_ref, *cur)
            update(sb_ref, *nxt)
        return cur

    lax.fori_loop(0, n_steps // unroll, steps, (jnp.int32(0), jnp.int32(0)))

    for qi in range(nq):
        tile = slice(qi * tq, (qi + 1) * tq)
        acc0, acc1 = acc_ref[0, tile], acc_ref[1, tile]
        n0 = acc0 / pltpu.roll(acc0, MLA_V, 1)
        n1 = acc1 / pltpu.roll(acc1, MLA_V, 1)
        o_ref[0, tile, :] = jnp.where(lane < MLA_V, n0, n1).astype(o_ref.dtype)


def _mla_attention(q, k, v):
    bsz, length, _ = q.shape
    tq = _row_tile(length, 384)
    bias = _mla_bias_table(tq)
    seq = lambda w: pl.BlockSpec((1, length, w), lambda b, p: (b, 0, p))
    return pl.pallas_call(
        functools.partial(_mla_kernel, tq=tq, nq=length // tq),
        grid=(bsz, MLA_HEADS // 2),
        in_specs=[seq(2 * MLA_QK_PAD), seq(2 * MLA_QK_PAD), seq(2 * LANES),
                  pl.BlockSpec(bias.shape, lambda b, p: (0, 0, 0))],
        out_specs=seq(LANES),
        out_shape=jax.ShapeDtypeStruct((bsz, length, D_MLA), jnp.bfloat16),
        scratch_shapes=[pltpu.VMEM((2, tq, tq), jnp.float32), pltpu.VMEM((2, tq, tq), jnp.float32),
                        pltpu.VMEM((2, tq, LANES), jnp.float32), pltpu.VMEM((2, length, LANES), jnp.float32)],
        compiler_params=_cparams("parallel", "parallel"),
        name="mla_attention",
    )(q, k, v, bias)


RET_UNROLL_MAX = 11


def _ret_kernel(q_ref, k_ref, v_ref, g_ref, dec_ref, o_ref, *, nblk):
    bf16 = jnp.bfloat16
    f32 = jnp.float32
    lane = lax.broadcasted_iota(jnp.int32, (BLOCK, LANES), 1)
    lane2 = lax.broadcasted_iota(jnp.int32, (BLOCK, 2 * RET_V), 1)
    feat = lax.broadcasted_iota(jnp.int32, (BLOCK, 2 * RET_V), 0)
    zero = jnp.zeros((BLOCK, LANES), bf16)
    zero2 = jnp.zeros((BLOCK, 2 * RET_V), bf16)
    d_in = jnp.concatenate([dec_ref[0, 0], dec_ref[1, 0]], axis=0)
    q_decay = jnp.concatenate([dec_ref[0, 1], dec_ref[1, 1]], axis=1)
    k_decay = jnp.where(lane < RET_QK, dec_ref[0, 2], dec_ref[1, 2])
    c_decay = jnp.concatenate([dec_ref[0, 3], dec_ref[1, 3]], axis=1)
    same_head = ((feat < RET_QK) == (lane2 < RET_V)).astype(f32)

    def chunks(ci, state):
        for u in range(unroll):
            state = chunk(ci * unroll + u, state)
        return state

    def chunk(c, state):
        rows = pl.ds(pl.multiple_of(c * BLOCK, BLOCK), BLOCK)
        qb = q_ref[0, rows, :]
        kb = k_ref[0, rows, :]
        vt = v_ref[0, rows, :]
        qstack = jnp.concatenate([jnp.where(lane < RET_QK, qb, zero), jnp.where(lane < RET_QK, zero, qb)], axis=0)
        inner = _dot_nt(qstack, kb) * d_in
        inner_cat = jnp.concatenate([inner[:BLOCK], inner[BLOCK:]], axis=1).astype(bf16)
        v_blocks = jnp.concatenate([jnp.where(lane2 < RET_V, vt, zero2), jnp.where(lane2 < RET_V, zero2, vt)], axis=0)
        y = _dot(inner_cat, v_blocks) + _dot(qb, state.astype(bf16)) * q_decay
        kd = (kb.astype(f32) * k_decay).astype(bf16)
        state = state * c_decay + _dot_tn(kd, vt) * same_head
        for hd in range(2):
            cols = slice(hd * RET_V, (hd + 1) * RET_V)
            yh = y[:, cols]
            mu = jnp.mean(yh, axis=-1, keepdims=True)
            yc = yh - mu
            var = jnp.mean(yc * yc, axis=-1, keepdims=True)
            yn = yc * lax.rsqrt(var + LN_EPS)
            gate = g_ref[0, rows, cols].astype(f32)
            o_ref[0, rows, cols] = (gate * jax.nn.sigmoid(gate) * yn).astype(o_ref.dtype)
        return state

    unroll = max(u for u in range(1, RET_UNROLL_MAX + 1) if nblk % u == 0)
    lax.fori_loop(0, nblk // unroll, chunks, jnp.zeros((BLOCK, 2 * RET_V), f32))


def _retention(q, k, v, g, dec):
    bsz, length, _ = v.shape
    qk = lambda: pl.BlockSpec((1, length, LANES), lambda b, p: (b, 0, p))
    vg = lambda: pl.BlockSpec((1, length, 2 * RET_V), lambda b, p: (b, 0, p))
    return pl.pallas_call(
        functools.partial(_ret_kernel, nblk=length // BLOCK),
        grid=(bsz, RET_HEADS // 2),
        in_specs=[qk(), qk(), vg(), vg(),
                  pl.BlockSpec((2, 4, BLOCK, LANES), lambda b, p: (p, 0, 0, 0))],
        out_specs=pl.BlockSpec((1, length, 2 * RET_V), lambda b, p: (b, 0, p)),
        out_shape=jax.ShapeDtypeStruct((bsz, length, D_RET), jnp.bfloat16),
        compiler_params=_cparams("parallel", "parallel"),
        name="retention",
    )(q, k, v, g, dec)


def _out_kernel(a_ref, b_ref, c_ref, h_ref, w_ref, g_ref, beta_ref, o_ref):
    bf16 = jnp.bfloat16
    mix = (_dot(a_ref[0].astype(bf16), w_ref[0:D_SB, :])
           + _dot(b_ref[0], w_ref[D_SB:D_SB + D_MLA, :])
           + _dot(c_ref[0], w_ref[D_SB + D_MLA:D_MIX, :]))
    o_ref[0] = _layer_norm_rows(DN_ALPHA * h_ref[0] + mix, g_ref[...], beta_ref[...])


def _out_projection(a, b, c, h, w, g, beta):
    bsz, length, d = h.shape
    tm = _row_tile(length, 704)
    tok = lambda wd: pl.BlockSpec((1, tm, wd), lambda i, j: (i, j, 0))
    const2 = lambda i, j: (0, 0)
    return pl.pallas_call(
        _out_kernel,
        grid=(bsz, length // tm),
        in_specs=[tok(D_SB), tok(D_MLA), tok(D_RET), tok(d),
                  pl.BlockSpec(w.shape, const2), pl.BlockSpec(g.shape, const2),
                  pl.BlockSpec(beta.shape, const2)],
        out_specs=tok(d),
        out_shape=jax.ShapeDtypeStruct(h.shape, jnp.float32),
        compiler_params=_cparams("parallel", "parallel"),
        name="out_projection",
    )(a, b, c, h, w, g, beta)


FF_CHUNK = 1024


def _mlp_kernel(h_ref, w1_ref, w2_ref, g_ref, beta_ref, o_ref):
    bf16 = jnp.bfloat16
    h = h_ref[0]
    hb = h.astype(bf16)
    acc = DN_ALPHA * h
    for lo in range(0, D_FF, FF_CHUNK):
        u = jnp.maximum(_dot(hb, w1_ref[:, lo:lo + FF_CHUNK]), 0.0)
        acc = acc + _dot((u * u).astype(bf16), w2_ref[lo:lo + FF_CHUNK, :])
    o_ref[0] = _layer_norm_rows(acc, g_ref[...], beta_ref[...])


def _mlp(h, w1, w2, g, beta):
    bsz, length, d = h.shape
    tm = _row_tile(length, 704)
    tok = pl.BlockSpec((1, tm, d), lambda i, j: (i, j, 0))
    const2 = lambda i, j: (0, 0)
    return pl.pallas_call(
        _mlp_kernel,
        grid=(bsz, length // tm),
        in_specs=[tok, pl.BlockSpec(w1.shape, const2), pl.BlockSpec(w2.shape, const2),
                  pl.BlockSpec(g.shape, const2), pl.BlockSpec(beta.shape, const2)],
        out_specs=tok,
        out_shape=jax.ShapeDtypeStruct(h.shape, jnp.float32),
        compiler_params=_cparams("parallel", "parallel"),
        name="mlp",
    )(h, w1, w2, g, beta)


def _rope_tables(length):
    pos = (jnp.arange(length) - N_PAD).astype(jnp.float32)

    def angles(half):
        inv = ROPE_THETA ** (-jnp.arange(half, dtype=jnp.float32) / half)
        ang = pos[:, None] * inv[None, :]
        return jnp.cos(ang), jnp.sin(ang)

    cos, sin = angles(MLA_ROPE // 2)
    one = jnp.ones((length, MLA_NOPE), jnp.float32)
    z = lambda w: jnp.zeros((length, w), jnp.float32)
    mla = jnp.stack([jnp.concatenate([one, cos, cos, z(32)], 1),
                     jnp.concatenate([z(64), -sin, z(16), z(32)], 1),
                     jnp.concatenate([z(64), z(16), sin, z(32)], 1)])
    cos, sin = angles(RET_QK // 2)
    ret = jnp.stack([jnp.concatenate([cos, cos] * 2, 1),
                     jnp.concatenate([-sin, z(32)] * 2, 1),
                     jnp.concatenate([z(32), sin] * 2, 1)])
    return mla, ret


def _retention_decays():
    log_g = jnp.log(jnp.array(RET_GAMMA, jnp.float32))
    idx = jnp.arange(BLOCK, dtype=jnp.float32)
    diff = idx[:, None] - idx[None, :]
    d_in = jnp.where(diff[None] >= 0, jnp.exp(jnp.maximum(diff, 0.0)[None] * log_g[:, None, None]), 0.0)
    q_decay = jnp.exp((idx[None, :] + 1.0) * log_g[:, None])
    k_decay = jnp.exp((BLOCK - 1.0 - idx[None, :]) * log_g[:, None])
    c_decay = jnp.exp(BLOCK * log_g)
    rep = lambda a: jnp.broadcast_to(a[:, :, None], (RET_HEADS, BLOCK, LANES))
    full = jnp.broadcast_to(c_decay[:, None, None], (RET_HEADS, BLOCK, LANES))
    return jnp.stack([d_in, rep(q_decay), rep(k_decay), full], axis=1)


def _suffix_sum_matrix():
    i = np.arange(BLOCK)
    upper = (i[:, None] > i[None, :]).astype(np.float32)
    ones = np.ones((BLOCK, BLOCK), np.float32)
    half = np.concatenate([upper, ones], axis=1)
    return jnp.asarray(np.concatenate([half, half], axis=0), jnp.bfloat16)


def kernel(x, meta_tokens, ln_emb_g, ln_emb_b, w_in, mla_q_norm, mla_kv_norm, w_uq, w_ukv, w_out,
           ln1_g, ln1_b, w_ff1, w_ff2, ln2_g, ln2_b):
    bsz, seq, d = x.shape
    assert d == D_MODEL
    bf16 = jnp.bfloat16
    depth = w_in.shape[0]

    meta = jnp.broadcast_to(meta_tokens[None].astype(x.dtype), (bsz, N_META, d))
    h = jnp.concatenate([jnp.zeros((bsz, N_PAD, d), x.dtype), meta, x], axis=1)
    length = h.shape[1]
    assert length % BLOCK == 0

    zc = lambda n: jnp.zeros((depth, d, n), w_in.dtype)
    k_r_lo = 3 * D_SB + MLA_Q_LORA + MLA_KV_LORA
    win = jnp.concatenate([w_in[..., :k_r_lo], zc(MLA_NOPE), w_in[..., k_r_lo:k_r_lo + MLA_ROPE],
                           zc(LANES - MLA_NOPE - MLA_ROPE), w_in[..., k_r_lo + MLA_ROPE:]], axis=-1).astype(bf16)
    assert win.shape[-1] == N_IN_PAD
    wuq = w_uq.reshape(depth, MLA_Q_LORA, MLA_HEADS, MLA_NOPE + MLA_ROPE)
    wuq = jnp.pad(wuq, ((0, 0), (0, 0), (0, 0), (0, MLA_QK_PAD - MLA_NOPE - MLA_ROPE)))
    wuq = wuq.reshape(depth, MLA_Q_LORA, MLA_HEADS * MLA_QK_PAD).astype(bf16)
    wukv = w_ukv.reshape(depth, MLA_KV_LORA, MLA_HEADS, MLA_NOPE + MLA_V)
    wuk = jnp.pad(wukv[..., :MLA_NOPE], ((0, 0), (0, 0), (0, 0), (0, MLA_QK_PAD - MLA_NOPE)))
    wuk = wuk.reshape(depth, MLA_KV_LORA, MLA_HEADS * MLA_QK_PAD).astype(bf16)
    wuv = wukv[..., MLA_NOPE:].reshape(depth, MLA_KV_LORA, D_MLA).astype(bf16)
    wout = w_out.astype(bf16)
    w1 = w_ff1.astype(bf16)
    w2 = w_ff2.astype(bf16)

    mtab, rtab = _rope_tables(length)
    dec = _retention_decays()
    uo = _suffix_sum_matrix()
    row2 = lambda a: a.reshape(1, -1)

    h = _embed_layer_norm(h, ln_emb_g, ln_emb_b)
    for l in range(depth):
        sbq, sbk, sbv, mq, mk, mv, rq, rk, rv, rg = _projections(
            h, win[l], row2(mla_q_norm[l]), row2(mla_kv_norm[l]), wuq[l], wuk[l], wuv[l], mtab, rtab)
        out_a = _stick_breaking(sbq, sbk, sbv, uo)
        out_b = _mla_attention(mq, mk, mv)
        out_c = _retention(rq, rk, rv, rg, dec)
        h = _out_projection(out_a, out_b, out_c, h, wout[l], row2(ln1_g[l]), row2(ln1_b[l]))
        h = _mlp(h, w1[l], w2[l], row2(ln2_g[l]), row2(ln2_b[l]))
    return h[:, N_PAD + N_META:]
```

```python
import functools

import numpy as np
import jax
import jax.numpy as jnp
from jax import lax
from jax.experimental import pallas as pl
from jax.experimental.pallas import tpu as pltpu

D_MODEL = 1024
DEPTH = 4
N_META = 16
BLOCK = 128
N_PAD = (-N_META) % BLOCK

SB_HEADS, SB_HEAD_DIM = 8, 64
MLA_HEADS, MLA_NOPE, MLA_ROPE, MLA_V = 8, 64, 32, 64
MLA_Q_LORA, MLA_KV_LORA = 384, 256
RET_HEADS, RET_QK, RET_V = 4, 64, 128

D_SB = SB_HEADS * SB_HEAD_DIM
D_MLA = MLA_HEADS * MLA_V
D_RET = RET_HEADS * RET_V
D_MIX = D_SB + D_MLA + D_RET
D_FF = 4 * D_MODEL
D_RQK = RET_HEADS * RET_QK

ROPE_THETA = 10000.0
LN_EPS = 1e-5
DN_ALPHA = (2 * DEPTH) ** 0.25
RET_GAMMA = tuple(1.0 - 2.0 ** (-5 - h) for h in range(RET_HEADS))

LANES = 128
MLA_QK_PAD = LANES

C_SBQ, C_SBK, C_SBV = 0, D_SB, 2 * D_SB
C_CQ = 3 * D_SB
C_CKV = C_CQ + MLA_Q_LORA
C_KR = C_CKV + MLA_KV_LORA
C_RQ = C_KR + LANES
C_RK = C_RQ + D_RQK
C_RV = C_RK + D_RQK
C_RG = C_RV + D_RET
N_IN_PAD = C_RG + D_RET

VMEM_LIMIT_BYTES = 56 * 1024 * 1024

LOG2E = 1.4426950408889634
SB_LOG_CUTOFF = 105.0
SB_LOG2_CUTOFF = SB_LOG_CUTOFF * LOG2E
SB_GROUP_MAX = 11
NEG_BIG = -1e30


def _row_tile(length, cap):
    best = None
    for t in range(16, min(length, cap) + 1, 16):
        if length % t == 0:
            best = t
    assert best is not None, length
    return best


def _cparams(*sem):
    return pltpu.CompilerParams(dimension_semantics=sem, vmem_limit_bytes=VMEM_LIMIT_BYTES)


def _layer_weight_spec(stacked, layer):
    return pl.BlockSpec((None,) + stacked.shape[1:], lambda i, j: (layer, 0, 0))


def _layer_norm_rows(x, g, b):
    mu = jnp.mean(x, axis=-1, keepdims=True)
    xc = x - mu
    var = jnp.mean(xc * xc, axis=-1, keepdims=True)
    return xc * lax.rsqrt(var + LN_EPS) * g + b


def _rms_norm_rows(x, g):
    return x * lax.rsqrt(jnp.mean(x * x, axis=-1, keepdims=True) + LN_EPS) * g


def _dot(a, b):
    return jnp.dot(a, b, preferred_element_type=jnp.float32)


def _dot_nt(a, b):
    return lax.dot_general(a, b, (((1,), (1,)), ((), ())), preferred_element_type=jnp.float32)


def _dot_tn(a, b):
    return lax.dot_general(a, b, (((0,), (0,)), ((), ())), preferred_element_type=jnp.float32)


def _ln_kernel(x_ref, g_ref, b_ref, o_ref):
    o_ref[...] = _layer_norm_rows(x_ref[...], g_ref[...], b_ref[...])


def _embed_layer_norm(h, g, b):
    bsz, length, d = h.shape
    tm = _row_tile(length, 1408)
    return pl.pallas_call(
        _ln_kernel,
        grid=(bsz, length // tm),
        in_specs=[pl.BlockSpec((1, tm, d), lambda i, j: (i, j, 0)),
                  pl.BlockSpec((1, 1, d), lambda i, j: (0, 0, 0)),
                  pl.BlockSpec((1, 1, d), lambda i, j: (0, 0, 0))],
        out_specs=pl.BlockSpec((1, tm, d), lambda i, j: (i, j, 0)),
        out_shape=jax.ShapeDtypeStruct(h.shape, jnp.float32),
        compiler_params=_cparams("parallel", "parallel"),
        name="embed_ln",
    )(h, g.reshape(1, 1, d), b.reshape(1, 1, d))


def _rope_tile(x, tab_ref, half):
    c, s1, s2 = tab_ref[0], tab_ref[1], tab_ref[2]
    return x * c + pltpu.roll(x, LANES - half, 1) * s1 + pltpu.roll(x, half, 1) * s2


def _proj_kernel(h_ref, win_ref, qn_ref, kvn_ref, wuq_ref, wuk_ref, wuv_ref, mtab_ref, rtab_ref,
                 sbq_ref, sbk_ref, sbv_ref, mq_ref, mk_ref, mv_ref, rq_ref, rk_ref, rv_ref, rg_ref,
                 *, tm):
    bf16 = jnp.bfloat16
    hb = h_ref[0].astype(bf16)

    groups = {}

    def seg(lo, hi):
        for g_lo, g_hi in ((C_SBQ, C_CQ), (C_CQ, C_RQ), (C_RQ, N_IN_PAD)):
            if g_lo <= lo and hi <= g_hi:
                if g_lo not in groups:
                    groups[g_lo] = _dot(hb, win_ref[:, g_lo:g_hi])
                return groups[g_lo][:, lo - g_lo:hi - g_lo]
        raise ValueError((lo, hi))

    sbq_ref[0] = (seg(C_SBQ, C_SBK) * (LOG2E * SB_HEAD_DIM ** -0.5)).astype(bf16)
    sbk_ref[0] = seg(C_SBK, C_SBV).astype(bf16)
    sbv_ref[0] = seg(C_SBV, C_CQ).astype(bf16)

    cq = _rms_norm_rows(seg(C_CQ, C_CKV), qn_ref[...])
    q = _dot(cq.astype(bf16), wuq_ref[...])
    mla_scale = LOG2E * (MLA_NOPE + MLA_ROPE) ** -0.5
    for hd in range(MLA_HEADS):
        lo = hd * MLA_QK_PAD
        blk = _rope_tile(q[:, lo:lo + MLA_QK_PAD], mtab_ref, MLA_ROPE // 2)
        mq_ref[0, :, lo:lo + MLA_QK_PAD] = (blk * mla_scale).astype(bf16)
    ckv = _rms_norm_rows(seg(C_CKV, C_KR), kvn_ref[...]).astype(bf16)
    kn = _dot(ckv, wuk_ref[...])
    kr = _rope_tile(seg(C_KR, C_RQ), mtab_ref, MLA_ROPE // 2)
    for hd in range(MLA_HEADS):
        lo = hd * MLA_QK_PAD
        mk_ref[0, :, lo:lo + MLA_QK_PAD] = (kn[:, lo:lo + MLA_QK_PAD] + kr).astype(bf16)
    mv = _dot(ckv, wuv_ref[...])
    lane = lax.broadcasted_iota(jnp.int32, (tm, LANES), 1)
    for hd in range(MLA_HEADS):
        pair_tile = mv[:, (hd // 2) * LANES:(hd // 2 + 1) * LANES]
        own = (lane < MLA_V) if hd % 2 == 0 else (lane >= MLA_V)
        mv_ref[0, :, hd * LANES:(hd + 1) * LANES] = jnp.where(own, pair_tile, 1.0).astype(bf16)

    row = lax.broadcasted_iota(jnp.int32, (tm, LANES), 0) + pl.program_id(1) * tm
    kscale = jnp.where(row >= N_PAD, RET_QK ** -0.5, 0.0).astype(jnp.float32)
    rq = seg(C_RQ, C_RK)
    rk = seg(C_RK, C_RV)
    for grp in range(D_RQK // LANES):
        lo = grp * LANES
        rq_ref[0, :, lo:lo + LANES] = _rope_tile(rq[:, lo:lo + LANES], rtab_ref, RET_QK // 2).astype(bf16)
        rk_ref[0, :, lo:lo + LANES] = (_rope_tile(rk[:, lo:lo + LANES], rtab_ref, RET_QK // 2)
                                       * kscale).astype(bf16)
    rv_ref[0] = seg(C_RV, C_RG).astype(bf16)
    rg_ref[0] = seg(C_RG, N_IN_PAD).astype(bf16)


def _projections(h, win, qn, kvn, wuq, wuk, wuv, mtab, rtab, layer):
    bsz, length, d = h.shape
    tm = _row_tile(length, 704)
    const2 = lambda i, j: (0, 0)
    wspec = functools.partial(_layer_weight_spec, layer=layer)
    tok = lambda w: pl.BlockSpec((1, tm, w), lambda i, j: (i, j, 0))
    widths = (D_SB, D_SB, D_SB, MLA_HEADS * MLA_QK_PAD, MLA_HEADS * MLA_QK_PAD, MLA_HEADS * LANES,
              D_RQK, D_RQK, D_RET, D_RET)
    return pl.pallas_call(
        functools.partial(_proj_kernel, tm=tm),
        grid=(bsz, length // tm),
        in_specs=[tok(d),
                  wspec(win),
                  pl.BlockSpec(qn.shape, const2),
                  pl.BlockSpec(kvn.shape, const2),
                  wspec(wuq), wspec(wuk), wspec(wuv),
                  pl.BlockSpec((3, tm, LANES), lambda i, j: (0, j, 0)),
                  pl.BlockSpec((3, tm, LANES), lambda i, j: (0, j, 0))],
        out_specs=[tok(w) for w in widths],
        out_shape=[jax.ShapeDtypeStruct((bsz, length, w), jnp.bfloat16) for w in widths],
        compiler_params=_cparams("parallel", "parallel"),
        name="projections",
    )(h, win, qn, kvn, wuq, wuk, wuv, mtab, rtab)


SB_STACK = 2 * BLOCK
SB_LIGHT_ROWS = 32


def _sb_step(qstack, kbs, vbs, r_prev, mask, uo):
    bf16 = jnp.bfloat16
    stack = qstack[0].shape[0]
    half = stack // 2
    head0 = lax.broadcasted_iota(jnp.int32, (half, LANES), 1) < SB_HEAD_DIM
    z = jnp.concatenate([_dot_nt(q, kb) for q, kb in zip(qstack, kbs)], axis=0)
    sp = jnp.log(1.0 + jnp.exp2(-jnp.abs(z))) * LOG2E
    lb = jnp.minimum(z, 0.0) - sp
    lk = lb - z
    if mask is not None:
        lk = jnp.where(mask, lk, 0.0)
    hi = lk.astype(bf16)
    mid = (lk - hi.astype(jnp.float32)).astype(bf16)
    ex = _dot(jnp.concatenate([hi, mid], axis=1), uo)
    arg = lb + ex[:, :LANES]
    if r_prev is not None:
        arg = arg + r_prev
    w = jnp.exp2(arg)
    if mask is not None:
        w = jnp.where(mask, w, 0.0)
    wb = w.astype(bf16)
    r_new = ex[:, LANES:] if r_prev is None else r_prev + ex[:, LANES:]
    outs = []
    for g, vb in enumerate(vbs):
        pv = _dot(wb[g * stack:(g + 1) * stack], vb)
        outs.append(jnp.where(head0, pv[:half], pv[half:]))
    return outs, r_new


def _sb_kernel(q_ref, k_ref, v_ref, uo_ref, o_ref, r_ref, rmax_ref, *, nblk, grp):
    uo = uo_ref[...]
    head0 = lax.broadcasted_iota(jnp.int32, (BLOCK, LANES), 1) < SB_HEAD_DIM
    zero = jnp.zeros((BLOCK, LANES), jnp.bfloat16)
    rows_all = grp * SB_STACK
    col = lax.broadcasted_iota(jnp.int32, (rows_all, LANES), 1)
    row = lax.broadcasted_iota(jnp.int32, (rows_all, LANES), 0)
    light = SB_LIGHT_ROWS
    starts = [g * SB_STACK + h * BLOCK for g in range(grp) for h in range(2)]
    lead = [pl.ds(s, light) for s in starts]
    lead_static = [slice(s, s + light) for s in starts]
    rest_static = [slice(s + light, s + BLOCK) for s in starts]

    def per_stream(values, stack=SB_STACK):
        return jnp.concatenate([jnp.full((stack, LANES), v, jnp.int32) for v in values], axis=0)

    def group(gi, carry):
        blocks = [gi * grp + g for g in range(grp)]
        rows = [pl.ds(pl.multiple_of(i * BLOCK, BLOCK), BLOCK) for i in blocks]
        qstack, qlight = [], []
        for g in range(grp):
            qb = q_ref[0, rows[g], :]
            q0, q1 = jnp.where(head0, qb, zero), jnp.where(head0, zero, qb)
            qstack.append(jnp.concatenate([q0, q1], axis=0))
            qlight.append(jnp.concatenate([q0[:light], q1[:light]], axis=0))

        first_valid = per_stream([jnp.where(b == 0, N_PAD, 0) for b in blocks])
        mask = (col < (row % BLOCK)) & (col >= first_valid)
        outs, r = _sb_step(qstack, [k_ref[0, rw, :] for rw in rows], [v_ref[0, rw, :] for rw in rows],
                           None, mask, uo)
        r_ref[...] = r
        for g in range(grp):
            o_ref[0, rows[g], :] = outs[g]

        def key_blocks(d):
            js = [b - d for b in blocks]
            krows = [pl.ds(pl.multiple_of(jnp.maximum(j, 0) * BLOCK, BLOCK), BLOCK) for j in js]
            return js, [k_ref[0, rw, :] for rw in krows], [v_ref[0, rw, :] for rw in krows]

        def masks(js, stack):
            fv = per_stream([jnp.where(j < 0, BLOCK, jnp.where(j == 0, N_PAD, 0)) for j in js], stack)
            keep = lax.broadcasted_iota(jnp.int32, (grp * stack, LANES), 1) >= fv
            more = per_stream([(j >= 1).astype(jnp.int32) for j in js], stack) > 0
            return keep, more

        def full_step(d, with_mask):
            js, kbs, vbs = key_blocks(d)
            keep, more = masks(js, SB_STACK) if with_mask else (None, None)
            outs, r_new = _sb_step(qstack, kbs, vbs, r_ref[...], keep, uo)
            r_ref[...] = r_new
            for g in range(grp):
                o_ref[0, rows[g], :] += outs[g]
            if with_mask:
                r_new = jnp.where(more, r_new, -jnp.inf)
            rmax_ref[0] = jnp.max(functools.reduce(jnp.maximum, [r_new[sl] for sl in lead_static]))
            rmax_ref[1] = jnp.max(functools.reduce(jnp.maximum, [r_new[sl] for sl in rest_static]))

        def light_step(d, with_mask):
            js, kbs, vbs = key_blocks(d)
            keep, more = masks(js, 2 * light) if with_mask else (None, None)
            r_prev = jnp.concatenate([r_ref[sl, :] for sl in lead], axis=0)
            outs, r_new = _sb_step(qlight, kbs, vbs, r_prev, keep, uo)
            for i, sl in enumerate(lead):
                r_ref[sl, :] = r_new[i * light:(i + 1) * light]
            for g in range(grp):
                lead_rows = pl.ds(pl.multiple_of(blocks[g] * BLOCK, BLOCK), light)
                o_ref[0, lead_rows, :] += outs[g]
            if with_mask:
                r_new = jnp.where(more, r_new, -jnp.inf)
            rmax_ref[0] = jnp.max(r_new)

        def step(c):
            d, _, full = c
            plain = blocks[0] - d >= 1
            for is_full, is_plain in ((True, True), (False, True), (True, False), (False, False)):
                @pl.when((full == is_full) & (plain == is_plain))
                def _():
                    (full_step if is_full else light_step)(d, not is_plain)

            full = rmax_ref[1] > -SB_LOG2_CUTOFF
            return d + 1, full | (rmax_ref[0] > -SB_LOG2_CUTOFF), full

        lax.while_loop(lambda c: c[1], step, (jnp.int32(1), blocks[-1] >= 1, jnp.bool_(True)))
        return carry

    lax.fori_loop(0, nblk // grp, group, 0)


def _stick_breaking(q, k, v, uo):
    bsz, length, _ = q.shape
    nblk = length // BLOCK
    grp = max(g for g in range(1, SB_GROUP_MAX + 1) if nblk % g == 0)
    pair = lambda: pl.BlockSpec((1, length, LANES), lambda b, p: (b, 0, p))
    return pl.pallas_call(
        functools.partial(_sb_kernel, nblk=nblk, grp=grp),
        grid=(bsz, D_SB // LANES),
        in_specs=[pair(), pair(), pair(), pl.BlockSpec(uo.shape, lambda b, p: (0, 0))],
        out_specs=pl.BlockSpec((1, length, LANES), lambda b, p: (b, 0, p)),
        out_shape=jax.ShapeDtypeStruct((bsz, length, D_SB), jnp.float32),
        scratch_shapes=[pltpu.VMEM((grp * SB_STACK, LANES), jnp.float32), pltpu.SMEM((2,), jnp.float32)],
        compiler_params=_cparams("parallel", "parallel"),
        name="stick_breaking",
    )(q, k, v, uo)


MLA_BIAS_DIAG, MLA_BIAS_DIAG_FIRST, MLA_BIAS_PAD, MLA_BIAS_NONE = range(4)
MLA_UNROLL_MAX = 22


def _mla_bias_table(tq):
    row = np.arange(tq)[:, None]
    col = np.arange(tq)[None, :]
    keep = np.stack([col <= row,
                     (col <= row) & ((col >= N_PAD) | (col == row)),
                     np.broadcast_to(col >= N_PAD, (tq, tq)),
                     np.ones((tq, tq), bool)])
    return jnp.asarray(np.where(keep, 0.0, NEG_BIG), jnp.float32)


def _mla_kernel(q_ref, k_ref, v_ref, bias_ref, o_ref, sa_ref, sb_ref, m_ref, acc_ref, *, tq, nq):
    bf16 = jnp.bfloat16
    lane = lax.broadcasted_iota(jnp.int32, (tq, LANES), 1)
    n_steps = nq * (nq + 1) // 2
    assert n_steps % 2 == 0, "the two logits buffers alternate, so steps are taken in pairs"
    unroll = max(u for u in range(2, MLA_UNROLL_MAX + 1, 2) if n_steps % u == 0)

    def rows(i):
        return pl.ds(pl.multiple_of(i * tq, tq), tq)

    def key_tile(qi, pos):
        return jnp.where(pos == 0, qi, pos - 1)

    def advance(qi, pos):
        last = pos == qi
        return jnp.where(last, jnp.minimum(qi + 1, nq - 1), qi), jnp.where(last, 0, pos + 1)

    def write_scores(buf, qi, pos):
        kt = key_tile(qi, pos)
        kind = jnp.where(pos == 0, jnp.where(qi == 0, MLA_BIAS_DIAG_FIRST, MLA_BIAS_DIAG),
                         jnp.where(kt == 0, MLA_BIAS_PAD, MLA_BIAS_NONE))
        for hd in range(2):
            cols = slice(hd * MLA_QK_PAD, (hd + 1) * MLA_QK_PAD)
            buf[hd] = _dot_nt(q_ref[0, rows(qi), cols], k_ref[0, rows(kt), cols]) + bias_ref[kind]

    def update(buf, qi, pos):
        kt = key_tile(qi, pos)
        for hd in range(2):
            m = jnp.where(pos == 0, NEG_BIG, m_ref[hd])
            s = buf[hd]
            m_new = jnp.maximum(m, jnp.max(s, axis=-1, keepdims=True))
            p = jnp.concatenate([jnp.exp2(s[:, c:c + LANES] - m_new) for c in range(0, tq, LANES)], axis=1)
            pv = _dot(p.astype(bf16), v_ref[0, rows(kt), hd * LANES:(hd + 1) * LANES])
            acc_ref[hd, rows(qi)] = jnp.exp2(m - m_new) * acc_ref[hd, rows(qi)] + pv
            m_ref[hd] = m_new

    acc_ref[...] = jnp.zeros(acc_ref.shape, jnp.float32)
    m_ref[...] = jnp.full(m_ref.shape, NEG_BIG, jnp.float32)
    write_scores(sa_ref, 0, 0)

    def steps(i, cur):
        for _ in range(unroll // 2):
            nxt = advance(*cur)
            write_scores(sb_ref, *nxt)
            update(sa_ref, *cur)
            cur = advance(*nxt)
            write_scores(sa_ref, *cur)
            update(sb_ref, *nxt)
        return cur

    lax.fori_loop(0, n_steps // unroll, steps, (jnp.int32(0), jnp.int32(0)))

    for qi in range(nq):
        tile = slice(qi * tq, (qi + 1) * tq)
        acc0, acc1 = acc_ref[0, tile], acc_ref[1, tile]
        n0 = acc0 / pltpu.roll(acc0, MLA_V, 1)
        n1 = acc1 / pltpu.roll(acc1, MLA_V, 1)
        o_ref[0, tile, :] = jnp.where(lane < MLA_V, n0, n1).astype(o_ref.dtype)


def _mla_attention(q, k, v):
    bsz, length, _ = q.shape
    tq = _row_tile(length, 384)
    bias = _mla_bias_table(tq)
    seq = lambda w: pl.BlockSpec((1, length, w), lambda b, p: (b, 0, p))
    return pl.pallas_call(
        functools.partial(_mla_kernel, tq=tq, nq=length // tq),
        grid=(bsz, MLA_HEADS // 2),
        in_specs=[seq(2 * MLA_QK_PAD), seq(2 * MLA_QK_PAD), seq(2 * LANES),
                  pl.BlockSpec(bias.shape, lambda b, p: (0, 0, 0))],
        out_specs=seq(LANES),
        out_shape=jax.ShapeDtypeStruct((bsz, length, D_MLA), jnp.bfloat16),
        scratch_shapes=[pltpu.VMEM((2, tq, tq), jnp.float32), pltpu.VMEM((2, tq, tq), jnp.float32),
                        pltpu.VMEM((2, tq, LANES), jnp.float32), pltpu.VMEM((2, length, LANES), jnp.float32)],
        compiler_params=_cparams("parallel", "parallel"),
        name="mla_attention",
    )(q, k, v, bias)


RET_UNROLL_MAX = 11


def _ret_kernel(q_ref, k_ref, v_ref, g_ref, dec_ref, o_ref, *, nblk):
    bf16 = jnp.bfloat16
    f32 = jnp.float32
    lane = lax.broadcasted_iota(jnp.int32, (BLOCK, LANES), 1)
    lane2 = lax.broadcasted_iota(jnp.int32, (BLOCK, 2 * RET_V), 1)
    feat = lax.broadcasted_iota(jnp.int32, (BLOCK, 2 * RET_V), 0)
    zero = jnp.zeros((BLOCK, LANES), bf16)
    zero2 = jnp.zeros((BLOCK, 2 * RET_V), bf16)
    d_in = jnp.concatenate([dec_ref[0, 0], dec_ref[1, 0]], axis=0)
    q_decay = jnp.concatenate([dec_ref[0, 1], dec_ref[1, 1]], axis=1)
    k_decay = jnp.where(lane < RET_QK, dec_ref[0, 2], dec_ref[1, 2])
    c_decay = jnp.concatenate([dec_ref[0, 3], dec_ref[1, 3]], axis=1)
    same_head = ((feat < RET_QK) == (lane2 < RET_V)).astype(f32)

    def chunks(ci, state):
        for u in range(unroll):
            state = chunk(ci * unroll + u, state)
        return state

    def chunk(c, state):
        rows = pl.ds(pl.multiple_of(c * BLOCK, BLOCK), BLOCK)
        qb = q_ref[0, rows, :]
        kb = k_ref[0, rows, :]
        vt = v_ref[0, rows, :]
        qstack = jnp.concatenate([jnp.where(lane < RET_QK, qb, zero), jnp.where(lane < RET_QK, zero, qb)], axis=0)
        inner = _dot_nt(qstack, kb) * d_in
        inner_cat = jnp.concatenate([inner[:BLOCK], inner[BLOCK:]], axis=1).astype(bf16)
        v_blocks = jnp.concatenate([jnp.where(lane2 < RET_V, vt, zero2), jnp.where(lane2 < RET_V, zero2, vt)], axis=0)
        y = _dot(inner_cat, v_blocks) + _dot(qb, state.astype(bf16)) * q_decay
        kd = (kb.astype(f32) * k_decay).astype(bf16)
        state = state * c_decay + _dot_tn(kd, vt) * same_head
        for hd in range(2):
            cols = slice(hd * RET_V, (hd + 1) * RET_V)
            yh = y[:, cols]
            mu = jnp.mean(yh, axis=-1, keepdims=True)
            yc = yh - mu
            var = jnp.mean(yc * yc, axis=-1, keepdims=True)
            yn = yc * lax.rsqrt(var + LN_EPS)
            gate = g_ref[0, rows, cols].astype(f32)
            o_ref[0, rows, cols] = (gate * jax.nn.sigmoid(gate) * yn).astype(o_ref.dtype)
        return state

    unroll = max(u for u in range(1, RET_UNROLL_MAX + 1) if nblk % u == 0)
    lax.fori_loop(0, nblk // unroll, chunks, jnp.zeros((BLOCK, 2 * RET_V), f32))


def _retention(q, k, v, g, dec):
    bsz, length, _ = v.shape
    qk = lambda: pl.BlockSpec((1, length, LANES), lambda b, p: (b, 0, p))
    vg = lambda: pl.BlockSpec((1, length, 2 * RET_V), lambda b, p: (b, 0, p))
    return pl.pallas_call(
        functools.partial(_ret_kernel, nblk=length // BLOCK),
        grid=(bsz, RET_HEADS // 2),
        in_specs=[qk(), qk(), vg(), vg(),
                  pl.BlockSpec((2, 4, BLOCK, LANES), lambda b, p: (p, 0, 0, 0))],
        out_specs=pl.BlockSpec((1, length, 2 * RET_V), lambda b, p: (b, 0, p)),
        out_shape=jax.ShapeDtypeStruct((bsz, length, D_RET), jnp.bfloat16),
        compiler_params=_cparams("parallel", "parallel"),
        name="retention",
    )(q, k, v, g, dec)


def _out_kernel(a_ref, b_ref, c_ref, h_ref, w_ref, g_ref, beta_ref, o_ref):
    bf16 = jnp.bfloat16
    mix = (_dot(a_ref[0].astype(bf16), w_ref[0:D_SB, :])
           + _dot(b_ref[0], w_ref[D_SB:D_SB + D_MLA, :])
           + _dot(c_ref[0], w_ref[D_SB + D_MLA:D_MIX, :]))
    o_ref[0] = _layer_norm_rows(DN_ALPHA * h_ref[0] + mix, g_ref[...], beta_ref[...])


def _out_projection(a, b, c, h, w, g, beta, layer):
    bsz, length, d = h.shape
    tm = _row_tile(length, 704)
    tok = lambda wd: pl.BlockSpec((1, tm, wd), lambda i, j: (i, j, 0))
    const2 = lambda i, j: (0, 0)
    return pl.pallas_call(
        _out_kernel,
        grid=(bsz, length // tm),
        in_specs=[tok(D_SB), tok(D_MLA), tok(D_RET), tok(d),
                  _layer_weight_spec(w, layer), pl.BlockSpec(g.shape, const2),
                  pl.BlockSpec(beta.shape, const2)],
        out_specs=tok(d),
        out_shape=jax.ShapeDtypeStruct(h.shape, jnp.float32),
        compiler_params=_cparams("parallel", "parallel"),
        name="out_projection",
    )(a, b, c, h, w, g, beta)


FF_CHUNK = 1024


def _mlp_kernel(h_ref, w1_ref, w2_ref, g_ref, beta_ref, o_ref):
    bf16 = jnp.bfloat16
    h = h_ref[0]
    hb = h.astype(bf16)
    acc = DN_ALPHA * h
    for lo in range(0, D_FF, FF_CHUNK):
        u = jnp.maximum(_dot(hb, w1_ref[:, lo:lo + FF_CHUNK]), 0.0)
        acc = acc + _dot((u * u).astype(bf16), w2_ref[lo:lo + FF_CHUNK, :])
    o_ref[0] = _layer_norm_rows(acc, g_ref[...], beta_ref[...])


def _mlp(h, w1, w2, g, beta, layer):
    bsz, length, d = h.shape
    tm = _row_tile(length, 704)
    tok = pl.BlockSpec((1, tm, d), lambda i, j: (i, j, 0))
    const2 = lambda i, j: (0, 0)
    return pl.pallas_call(
        _mlp_kernel,
        grid=(bsz, length // tm),
        in_specs=[tok, _layer_weight_spec(w1, layer), _layer_weight_spec(w2, layer),
                  pl.BlockSpec(g.shape, const2), pl.BlockSpec(beta.shape, const2)],
        out_specs=tok,
        out_shape=jax.ShapeDtypeStruct(h.shape, jnp.float32),
        compiler_params=_cparams("parallel", "parallel"),
        name="mlp",
    )(h, w1, w2, g, beta)


def _rope_tables(length):
    pos = (jnp.arange(length) - N_PAD).astype(jnp.float32)

    def angles(half):
        inv = ROPE_THETA ** (-jnp.arange(half, dtype=jnp.float32) / half)
        ang = pos[:, None] * inv[None, :]
        return jnp.cos(ang), jnp.sin(ang)

    cos, sin = angles(MLA_ROPE // 2)
    one = jnp.ones((length, MLA_NOPE), jnp.float32)
    z = lambda w: jnp.zeros((length, w), jnp.float32)
    mla = jnp.stack([jnp.concatenate([one, cos, cos, z(32)], 1),
                     jnp.concatenate([z(64), -sin, z(16), z(32)], 1),
                     jnp.concatenate([z(64), z(16), sin, z(32)], 1)])
    cos, sin = angles(RET_QK // 2)
    ret = jnp.stack([jnp.concatenate([cos, cos] * 2, 1),
                     jnp.concatenate([-sin, z(32)] * 2, 1),
                     jnp.concatenate([z(32), sin] * 2, 1)])
    return mla, ret


def _retention_decays():
    log_g = jnp.log(jnp.array(RET_GAMMA, jnp.float32))
    idx = jnp.arange(BLOCK, dtype=jnp.float32)
    diff = idx[:, None] - idx[None, :]
    d_in = jnp.where(diff[None] >= 0, jnp.exp(jnp.maximum(diff, 0.0)[None] * log_g[:, None, None]), 0.0)
    q_decay = jnp.exp((idx[None, :] + 1.0) * log_g[:, None])
    k_decay = jnp.exp((BLOCK - 1.0 - idx[None, :]) * log_g[:, None])
    c_decay = jnp.exp(BLOCK * log_g)
    rep = lambda a: jnp.broadcast_to(a[:, :, None], (RET_HEADS, BLOCK, LANES))
    full = jnp.broadcast_to(c_decay[:, None, None], (RET_HEADS, BLOCK, LANES))
    return jnp.stack([d_in, rep(q_decay), rep(k_decay), full], axis=1)


def _suffix_sum_matrix():
    i = np.arange(BLOCK)
    upper = (i[:, None] > i[None, :]).astype(np.float32)
    ones = np.ones((BLOCK, BLOCK), np.float32)
    half = np.concatenate([upper, ones], axis=1)
    return jnp.asarray(np.concatenate([half, half], axis=0), jnp.bfloat16)


def kernel(x, meta_tokens, ln_emb_g, ln_emb_b, w_in, mla_q_norm, mla_kv_norm, w_uq, w_ukv, w_out,
           ln1_g, ln1_b, w_ff1, w_ff2, ln2_g, ln2_b):
    bsz, seq, d = x.shape
    assert d == D_MODEL
    bf16 = jnp.bfloat16
    depth = w_in.shape[0]

    meta = jnp.broadcast_to(meta_tokens[None].astype(x.dtype), (bsz, N_META, d))
    h = jnp.concatenate([jnp.zeros((bsz, N_PAD, d), x.dtype), meta, x], axis=1)
    length = h.shape[1]
    assert length % BLOCK == 0

    zc = lambda n: jnp.zeros((depth, d, n), w_in.dtype)
    k_r_lo = 3 * D_SB + MLA_Q_LORA + MLA_KV_LORA
    win = jnp.concatenate([w_in[..., :k_r_lo], zc(MLA_NOPE), w_in[..., k_r_lo:k_r_lo + MLA_ROPE],
                           zc(LANES - MLA_NOPE - MLA_ROPE), w_in[..., k_r_lo + MLA_ROPE:]], axis=-1).astype(bf16)
    assert win.shape[-1] == N_IN_PAD
    wuq = w_uq.reshape(depth, MLA_Q_LORA, MLA_HEADS, MLA_NOPE + MLA_ROPE)
    wuq = jnp.pad(wuq, ((0, 0), (0, 0), (0, 0), (0, MLA_QK_PAD - MLA_NOPE - MLA_ROPE)))
    wuq = wuq.reshape(depth, MLA_Q_LORA, MLA_HEADS * MLA_QK_PAD).astype(bf16)
    wukv = w_ukv.reshape(depth, MLA_KV_LORA, MLA_HEADS, MLA_NOPE + MLA_V)
    wuk = jnp.pad(wukv[..., :MLA_NOPE], ((0, 0), (0, 0), (0, 0), (0, MLA_QK_PAD - MLA_NOPE)))
    wuk = wuk.reshape(depth, MLA_KV_LORA, MLA_HEADS * MLA_QK_PAD).astype(bf16)
    wuv = wukv[..., MLA_NOPE:].reshape(depth, MLA_KV_LORA, D_MLA).astype(bf16)
    wout = w_out.astype(bf16)
    w1 = w_ff1.astype(bf16)
    w2 = w_ff2.astype(bf16)

    mtab, rtab = _rope_tables(length)
    dec = _retention_decays()
    uo = _suffix_sum_matrix()
    row2 = lambda a: a.reshape(1, -1)

    h = _embed_layer_norm(h, ln_emb_g, ln_emb_b)
    for l in range(depth):
        sbq, sbk, sbv, mq, mk, mv, rq, rk, rv, rg = _projections(
            h, win, row2(mla_q_norm[l]), row2(mla_kv_norm[l]), wuq, wuk, wuv, mtab, rtab, l)
        out_a = _stick_breaking(sbq, sbk, sbv, uo)
        out_b = _mla_attention(mq, mk, mv)
        out_c = _retention(rq, rk, rv, rg, dec)
        h = _out_projection(out_a, out_b, out_c, h, wout, row2(ln1_g[l]), row2(ln1_b[l]), l)
        h = _mlp(h, w1, w2, row2(ln2_g[l]), row2(ln2_b[l]), l)
    return h[:, N_PAD + N_META:]
```

```python
import functools

import numpy as np
import jax
import jax.numpy as jnp
from jax import lax
from jax.experimental import pallas as pl
from jax.experimental.pallas import tpu as pltpu

D_MODEL = 1024
DEPTH = 4
N_META = 16
BLOCK = 128
N_PAD = (-N_META) % BLOCK

SB_HEADS, SB_HEAD_DIM = 8, 64
MLA_HEADS, MLA_NOPE, MLA_ROPE, MLA_V = 8, 64, 32, 64
MLA_Q_LORA, MLA_KV_LORA = 384, 256
RET_HEADS, RET_QK, RET_V = 4, 64, 128

D_SB = SB_HEADS * SB_HEAD_DIM
D_MLA = MLA_HEADS * MLA_V
D_RET = RET_HEADS * RET_V
D_MIX = D_SB + D_MLA + D_RET
D_FF = 4 * D_MODEL
D_RQK = RET_HEADS * RET_QK

ROPE_THETA = 10000.0
LN_EPS = 1e-5
DN_ALPHA = (2 * DEPTH) ** 0.25
RET_GAMMA = tuple(1.0 - 2.0 ** (-5 - h) for h in range(RET_HEADS))

LANES = 128
MLA_QK_PAD = LANES

C_SBQ, C_SBK, C_SBV = 0, D_SB, 2 * D_SB
C_CQ = 3 * D_SB
C_CKV = C_CQ + MLA_Q_LORA
C_KR = C_CKV + MLA_KV_LORA
C_RQ = C_KR + LANES
C_RK = C_RQ + D_RQK
C_RV = C_RK + D_RQK
C_RG = C_RV + D_RET
N_IN_PAD = C_RG + D_RET

VMEM_LIMIT_BYTES = 56 * 1024 * 1024

LOG2E = 1.4426950408889634
SB_LOG_CUTOFF = 105.0
SB_LOG2_CUTOFF = SB_LOG_CUTOFF * LOG2E
SB_GROUP_MAX = 11
NEG_BIG = -1e30


def _row_tile(length, cap):
    best = None
    for t in range(16, min(length, cap) + 1, 16):
        if length % t == 0:
            best = t
    assert best is not None, length
    return best


def _cparams(*sem):
    return pltpu.CompilerParams(dimension_semantics=sem, vmem_limit_bytes=VMEM_LIMIT_BYTES)


def _layer_weight_spec(stacked, layer):
    return pl.BlockSpec((None,) + stacked.shape[1:], lambda i, j: (layer, 0, 0))


def _layer_norm_rows(x, g, b):
    mu = jnp.mean(x, axis=-1, keepdims=True)
    xc = x - mu
    var = jnp.mean(xc * xc, axis=-1, keepdims=True)
    return xc * lax.rsqrt(var + LN_EPS) * g + b


def _rms_norm_rows(x, g):
    return x * lax.rsqrt(jnp.mean(x * x, axis=-1, keepdims=True) + LN_EPS) * g


def _dot(a, b):
    return jnp.dot(a, b, preferred_element_type=jnp.float32)


def _dot_nt(a, b):
    return lax.dot_general(a, b, (((1,), (1,)), ((), ())), preferred_element_type=jnp.float32)


def _dot_tn(a, b):
    return lax.dot_general(a, b, (((0,), (0,)), ((), ())), preferred_element_type=jnp.float32)


def _embed_kernel(x_ref, meta_ref, g_ref, b_ref, o_ref, *, tm):
    head = N_PAD + N_META

    @pl.when(pl.program_id(1) == 0)
    def _():
        lead = jnp.concatenate([jnp.zeros((N_PAD, meta_ref.shape[1]), jnp.float32), meta_ref[...]], axis=0)
        rows = jnp.concatenate([lead, x_ref[0, :tm - head, :]], axis=0)
        o_ref[0] = _layer_norm_rows(rows, g_ref[...], b_ref[...])

    @pl.when(pl.program_id(1) > 0)
    def _():
        o_ref[0] = _layer_norm_rows(x_ref[0], g_ref[...], b_ref[...])


def _embed_layer_norm(x, meta_tokens, g, b):
    bsz, seq, d = x.shape
    head = N_PAD + N_META
    tm = _row_tile(seq, 1408)
    const2 = lambda i, j: (0, 0)
    out_rows = lambda i, j: (i, pl.multiple_of(jnp.where(j == 0, 0, head + (j - 1) * tm), BLOCK), 0)
    return pl.pallas_call(
        functools.partial(_embed_kernel, tm=tm),
        grid=(bsz, seq // tm + 1),
        in_specs=[pl.BlockSpec((1, tm, d), lambda i, j: (i, jnp.maximum(j - 1, 0), 0)),
                  pl.BlockSpec(meta_tokens.shape, const2),
                  pl.BlockSpec((1, d), const2), pl.BlockSpec((1, d), const2)],
        out_specs=pl.BlockSpec((pl.Element(1), pl.Element(tm), pl.Element(d)), out_rows),
        out_shape=jax.ShapeDtypeStruct((bsz, seq + head, d), jnp.float32),
        compiler_params=_cparams("parallel", "arbitrary"),
        name="embed_ln",
    )(x, meta_tokens.astype(jnp.float32), g.reshape(1, d), b.reshape(1, d))


def _rope_tile(x, tab_ref, half):
    c, s1, s2 = tab_ref[0], tab_ref[1], tab_ref[2]
    return x * c + pltpu.roll(x, LANES - half, 1) * s1 + pltpu.roll(x, half, 1) * s2


def _proj_kernel(h_ref, win_ref, qn_ref, kvn_ref, wuq_ref, wuk_ref, wuv_ref, mtab_ref, rtab_ref,
                 sbq_ref, sbk_ref, sbv_ref, mq_ref, mk_ref, mv_ref, rq_ref, rk_ref, rv_ref, rg_ref,
                 *, tm):
    bf16 = jnp.bfloat16
    hb = h_ref[0].astype(bf16)

    groups = {}

    def seg(lo, hi):
        for g_lo, g_hi in ((C_SBQ, C_CQ), (C_CQ, C_RQ), (C_RQ, N_IN_PAD)):
            if g_lo <= lo and hi <= g_hi:
                if g_lo not in groups:
                    groups[g_lo] = _dot(hb, win_ref[:, g_lo:g_hi])
                return groups[g_lo][:, lo - g_lo:hi - g_lo]
        raise ValueError((lo, hi))

    sbq_ref[0] = (seg(C_SBQ, C_SBK) * (LOG2E * SB_HEAD_DIM ** -0.5)).astype(bf16)
    sbk_ref[0] = seg(C_SBK, C_SBV).astype(bf16)
    sbv_ref[0] = seg(C_SBV, C_CQ).astype(bf16)

    cq = _rms_norm_rows(seg(C_CQ, C_CKV), qn_ref[...])
    q = _dot(cq.astype(bf16), wuq_ref[...])
    mla_scale = LOG2E * (MLA_NOPE + MLA_ROPE) ** -0.5
    for hd in range(MLA_HEADS):
        lo = hd * MLA_QK_PAD
        blk = _rope_tile(q[:, lo:lo + MLA_QK_PAD], mtab_ref, MLA_ROPE // 2)
        mq_ref[0, :, lo:lo + MLA_QK_PAD] = (blk * mla_scale).astype(bf16)
    ckv = _rms_norm_rows(seg(C_CKV, C_KR), kvn_ref[...]).astype(bf16)
    kn = _dot(ckv, wuk_ref[...])
    kr = _rope_tile(seg(C_KR, C_RQ), mtab_ref, MLA_ROPE // 2)
    for hd in range(MLA_HEADS):
        lo = hd * MLA_QK_PAD
        mk_ref[0, :, lo:lo + MLA_QK_PAD] = (kn[:, lo:lo + MLA_QK_PAD] + kr).astype(bf16)
    mv = _dot(ckv, wuv_ref[...])
    lane = lax.broadcasted_iota(jnp.int32, (tm, LANES), 1)
    for hd in range(MLA_HEADS):
        pair_tile = mv[:, (hd // 2) * LANES:(hd // 2 + 1) * LANES]
        own = (lane < MLA_V) if hd % 2 == 0 else (lane >= MLA_V)
        mv_ref[0, :, hd * LANES:(hd + 1) * LANES] = jnp.where(own, pair_tile, 1.0).astype(bf16)

    row = lax.broadcasted_iota(jnp.int32, (tm, LANES), 0) + pl.program_id(1) * tm
    kscale = jnp.where(row >= N_PAD, RET_QK ** -0.5, 0.0).astype(jnp.float32)
    rq = seg(C_RQ, C_RK)
    rk = seg(C_RK, C_RV)
    for grp in range(D_RQK // LANES):
        lo = grp * LANES
        rq_ref[0, :, lo:lo + LANES] = _rope_tile(rq[:, lo:lo + LANES], rtab_ref, RET_QK // 2).astype(bf16)
        rk_ref[0, :, lo:lo + LANES] = (_rope_tile(rk[:, lo:lo + LANES], rtab_ref, RET_QK // 2)
                                       * kscale).astype(bf16)
    rv_ref[0] = seg(C_RV, C_RG).astype(bf16)
    rg_ref[0] = seg(C_RG, N_IN_PAD).astype(bf16)


def _projections(h, win, qn, kvn, wuq, wuk, wuv, mtab, rtab, layer):
    bsz, length, d = h.shape
    tm = _row_tile(length, 704)
    const2 = lambda i, j: (0, 0)
    wspec = functools.partial(_layer_weight_spec, layer=layer)
    tok = lambda w: pl.BlockSpec((1, tm, w), lambda i, j: (i, j, 0))
    widths = (D_SB, D_SB, D_SB, MLA_HEADS * MLA_QK_PAD, MLA_HEADS * MLA_QK_PAD, MLA_HEADS * LANES,
              D_RQK, D_RQK, D_RET, D_RET)
    return pl.pallas_call(
        functools.partial(_proj_kernel, tm=tm),
        grid=(bsz, length // tm),
        in_specs=[tok(d),
                  wspec(win),
                  pl.BlockSpec(qn.shape, const2),
                  pl.BlockSpec(kvn.shape, const2),
                  wspec(wuq), wspec(wuk), wspec(wuv),
                  pl.BlockSpec((3, tm, LANES), lambda i, j: (0, j, 0)),
                  pl.BlockSpec((3, tm, LANES), lambda i, j: (0, j, 0))],
        out_specs=[tok(w) for w in widths],
        out_shape=[jax.ShapeDtypeStruct((bsz, length, w), jnp.bfloat16) for w in widths],
        compiler_params=_cparams("parallel", "parallel"),
        name="projections",
    )(h, win, qn, kvn, wuq, wuk, wuv, mtab, rtab)


SB_STACK = 2 * BLOCK
SB_LIGHT_ROWS = 32


def _sb_step(qstack, kbs, vbs, r_prev, mask, uo):
    bf16 = jnp.bfloat16
    stack = qstack[0].shape[0]
    half = stack // 2
    head0 = lax.broadcasted_iota(jnp.int32, (half, LANES), 1) < SB_HEAD_DIM
    z = jnp.concatenate([_dot_nt(q, kb) for q, kb in zip(qstack, kbs)], axis=0)
    sp = jnp.log(1.0 + jnp.exp2(-jnp.abs(z))) * LOG2E
    lb = jnp.minimum(z, 0.0) - sp
    lk = lb - z
    if mask is not None:
        lk = jnp.where(mask, lk, 0.0)
    hi = lk.astype(bf16)
    mid = (lk - hi.astype(jnp.float32)).astype(bf16)
    ex = _dot(jnp.concatenate([hi, mid], axis=1), uo)
    arg = lb + ex[:, :LANES]
    if r_prev is not None:
        arg = arg + r_prev
    w = jnp.exp2(arg)
    if mask is not None:
        w = jnp.where(mask, w, 0.0)
    wb = w.astype(bf16)
    r_new = ex[:, LANES:] if r_prev is None else r_prev + ex[:, LANES:]
    outs = []
    for g, vb in enumerate(vbs):
        pv = _dot(wb[g * stack:(g + 1) * stack], vb)
        outs.append(jnp.where(head0, pv[:half], pv[half:]))
    return outs, r_new


def _sb_kernel(q_ref, k_ref, v_ref, uo_ref, o_ref, r_ref, rmax_ref, *, nblk, grp):
    uo = uo_ref[...]
    head0 = lax.broadcasted_iota(jnp.int32, (BLOCK, LANES), 1) < SB_HEAD_DIM
    zero = jnp.zeros((BLOCK, LANES), jnp.bfloat16)
    rows_all = grp * SB_STACK
    col = lax.broadcasted_iota(jnp.int32, (rows_all, LANES), 1)
    row = lax.broadcasted_iota(jnp.int32, (rows_all, LANES), 0)
    light = SB_LIGHT_ROWS
    starts = [g * SB_STACK + h * BLOCK for g in range(grp) for h in range(2)]
    lead = [pl.ds(s, light) for s in starts]
    lead_static = [slice(s, s + light) for s in starts]
    rest_static = [slice(s + light, s + BLOCK) for s in starts]

    def per_stream(values, stack=SB_STACK):
        return jnp.concatenate([jnp.full((stack, LANES), v, jnp.int32) for v in values], axis=0)

    def group(gi, carry):
        blocks = [gi * grp + g for g in range(grp)]
        rows = [pl.ds(pl.multiple_of(i * BLOCK, BLOCK), BLOCK) for i in blocks]
        qstack, qlight = [], []
        for g in range(grp):
            qb = q_ref[0, rows[g], :]
            q0, q1 = jnp.where(head0, qb, zero), jnp.where(head0, zero, qb)
            qstack.append(jnp.concatenate([q0, q1], axis=0))
            qlight.append(jnp.concatenate([q0[:light], q1[:light]], axis=0))

        first_valid = per_stream([jnp.where(b == 0, N_PAD, 0) for b in blocks])
        mask = (col < (row % BLOCK)) & (col >= first_valid)
        outs, r = _sb_step(qstack, [k_ref[0, rw, :] for rw in rows], [v_ref[0, rw, :] for rw in rows],
                           None, mask, uo)
        r_ref[...] = r
        for g in range(grp):
            o_ref[0, rows[g], :] = outs[g]

        def key_blocks(d):
            js = [b - d for b in blocks]
            krows = [pl.ds(pl.multiple_of(jnp.maximum(j, 0) * BLOCK, BLOCK), BLOCK) for j in js]
            return js, [k_ref[0, rw, :] for rw in krows], [v_ref[0, rw, :] for rw in krows]

        def masks(js, stack):
            fv = per_stream([jnp.where(j < 0, BLOCK, jnp.where(j == 0, N_PAD, 0)) for j in js], stack)
            keep = lax.broadcasted_iota(jnp.int32, (grp * stack, LANES), 1) >= fv
            more = per_stream([(j >= 1).astype(jnp.int32) for j in js], stack) > 0
            return keep, more

        def full_step(d, with_mask):
            js, kbs, vbs = key_blocks(d)
            keep, more = masks(js, SB_STACK) if with_mask else (None, None)
            outs, r_new = _sb_step(qstack, kbs, vbs, r_ref[...], keep, uo)
            r_ref[...] = r_new
            for g in range(grp):
                o_ref[0, rows[g], :] += outs[g]
            if with_mask:
                r_new = jnp.where(more, r_new, -jnp.inf)
            rmax_ref[0] = jnp.max(functools.reduce(jnp.maximum, [r_new[sl] for sl in lead_static]))
            rmax_ref[1] = jnp.max(functools.reduce(jnp.maximum, [r_new[sl] for sl in rest_static]))

        def light_step(d, with_mask):
            js, kbs, vbs = key_blocks(d)
            keep, more = masks(js, 2 * light) if with_mask else (None, None)
            r_prev = jnp.concatenate([r_ref[sl, :] for sl in lead], axis=0)
            outs, r_new = _sb_step(qlight, kbs, vbs, r_prev, keep, uo)
            for i, sl in enumerate(lead):
                r_ref[sl, :] = r_new[i * light:(i + 1) * light]
            for g in range(grp):
                lead_rows = pl.ds(pl.multiple_of(blocks[g] * BLOCK, BLOCK), light)
                o_ref[0, lead_rows, :] += outs[g]
            if with_mask:
                r_new = jnp.where(more, r_new, -jnp.inf)
            rmax_ref[0] = jnp.max(r_new)

        def step(c):
            d, _, full = c
            plain = blocks[0] - d >= 1
            for is_full, is_plain in ((True, True), (False, True), (True, False), (False, False)):
                @pl.when((full == is_full) & (plain == is_plain))
                def _():
                    (full_step if is_full else light_step)(d, not is_plain)

            full = rmax_ref[1] > -SB_LOG2_CUTOFF
            return d + 1, full | (rmax_ref[0] > -SB_LOG2_CUTOFF), full

        lax.while_loop(lambda c: c[1], step, (jnp.int32(1), blocks[-1] >= 1, jnp.bool_(True)))
        return carry

    lax.fori_loop(0, nblk // grp, group, 0)


def _stick_breaking(q, k, v, uo):
    bsz, length, _ = q.shape
    nblk = length // BLOCK
    grp = max(g for g in range(1, SB_GROUP_MAX + 1) if nblk % g == 0)
    pair = lambda: pl.BlockSpec((1, length, LANES), lambda b, p: (b, 0, p))
    return pl.pallas_call(
        functools.partial(_sb_kernel, nblk=nblk, grp=grp),
        grid=(bsz, D_SB // LANES),
        in_specs=[pair(), pair(), pair(), pl.BlockSpec(uo.shape, lambda b, p: (0, 0))],
        out_specs=pl.BlockSpec((1, length, LANES), lambda b, p: (b, 0, p)),
        out_shape=jax.ShapeDtypeStruct((bsz, length, D_SB), jnp.float32),
        scratch_shapes=[pltpu.VMEM((grp * SB_STACK, LANES), jnp.float32), pltpu.SMEM((2,), jnp.float32)],
        compiler_params=_cparams("parallel", "parallel"),
        name="stick_breaking",
    )(q, k, v, uo)


MLA_BIAS_DIAG, MLA_BIAS_DIAG_FIRST, MLA_BIAS_PAD, MLA_BIAS_NONE = range(4)
MLA_UNROLL_MAX = 22


def _mla_bias_table(tq):
    row = np.arange(tq)[:, None]
    col = np.arange(tq)[None, :]
    keep = np.stack([col <= row,
                     (col <= row) & ((col >= N_PAD) | (col == row)),
                     np.broadcast_to(col >= N_PAD, (tq, tq)),
                     np.ones((tq, tq), bool)])
    return jnp.asarray(np.where(keep, 0.0, NEG_BIG), jnp.float32)


def _mla_kernel(q_ref, k_ref, v_ref, bias_ref, o_ref, sa_ref, sb_ref, m_ref, acc_ref, *, tq, nq):
    bf16 = jnp.bfloat16
    lane = lax.broadcasted_iota(jnp.int32, (tq, LANES), 1)
    n_steps = nq * (nq + 1) // 2
    assert n_steps % 2 == 0, "the two logits buffers alternate, so steps are taken in pairs"
    unroll = max(u for u in range(2, MLA_UNROLL_MAX + 1, 2) if n_steps % u == 0)

    def rows(i):
        return pl.ds(pl.multiple_of(i * tq, tq), tq)

    def key_tile(qi, pos):
        return jnp.where(pos == 0, qi, pos - 1)

    def advance(qi, pos):
        last = pos == qi
        return jnp.where(last, jnp.minimum(qi + 1, nq - 1), qi), jnp.where(last, 0, pos + 1)

    def write_scores(buf, qi, pos):
        kt = key_tile(qi, pos)
        kind = jnp.where(pos == 0, jnp.where(qi == 0, MLA_BIAS_DIAG_FIRST, MLA_BIAS_DIAG),
                         jnp.where(kt == 0, MLA_BIAS_PAD, MLA_BIAS_NONE))
        for hd in range(2):
            cols = slice(hd * MLA_QK_PAD, (hd + 1) * MLA_QK_PAD)
            buf[hd] = _dot_nt(q_ref[0, rows(qi), cols], k_ref[0, rows(kt), cols]) + bias_ref[kind]

    def update(buf, qi, pos):
        kt = key_tile(qi, pos)
        for hd in range(2):
            m = jnp.where(pos == 0, NEG_BIG, m_ref[hd])
            s = buf[hd]
            m_new = jnp.maximum(m, jnp.max(s, axis=-1, keepdims=True))
            p = jnp.concatenate([jnp.exp2(s[:, c:c + LANES] - m_new) for c in range(0, tq, LANES)], axis=1)
            pv = _dot(p.astype(bf16), v_ref[0, rows(kt), hd * LANES:(hd + 1) * LANES])
            acc_ref[hd, rows(qi)] = jnp.exp2(m - m_new) * acc_ref[hd, rows(qi)] + pv
            m_ref[hd] = m_new

    acc_ref[...] = jnp.zeros(acc_ref.shape, jnp.float32)
    m_ref[...] = jnp.full(m_ref.shape, NEG_BIG, jnp.float32)
    write_scores(sa_ref, 0, 0)

    def steps(i, cur):
        for _ in range(unroll // 2):
            nxt = advance(*cur)
            write_scores(sb_ref, *nxt)
            update(sa_ref, *cur)
            cur = advance(*nxt)
            write_scores(sa_ref, *cur)
            update(sb_ref, *nxt)
        return cur

    lax.fori_loop(0, n_steps // unroll, steps, (jnp.int32(0), jnp.int32(0)))

    for qi in range(nq):
        tile = slice(qi * tq, (qi + 1) * tq)
        acc0, acc1 = acc_ref[0, tile], acc_ref[1, tile]
        n0 = acc0 / pltpu.roll(acc0, MLA_V, 1)
        n1 = acc1 / pltpu.roll(acc1, MLA_V, 1)
        o_ref[0, tile, :] = jnp.where(lane < MLA_V, n0, n1).astype(o_ref.dtype)


def _mla_attention(q, k, v):
    bsz, length, _ = q.shape
    tq = _row_tile(length, 384)
    bias = _mla_bias_table(tq)
    seq = lambda w: pl.BlockSpec((1, length, w), lambda b, p: (b, 0, p))
    return pl.pallas_call(
        functools.partial(_mla_kernel, tq=tq, nq=length // tq),
        grid=(bsz, MLA_HEADS // 2),
        in_specs=[seq(2 * MLA_QK_PAD), seq(2 * MLA_QK_PAD), seq(2 * LANES),
                  pl.BlockSpec(bias.shape, lambda b, p: (0, 0, 0))],
        out_specs=seq(LANES),
        out_shape=jax.ShapeDtypeStruct((bsz, length, D_MLA), jnp.bfloat16),
        scratch_shapes=[pltpu.VMEM((2, tq, tq), jnp.float32), pltpu.VMEM((2, tq, tq), jnp.float32),
                        pltpu.VMEM((2, tq, LANES), jnp.float32), pltpu.VMEM((2, length, LANES), jnp.float32)],
        compiler_params=_cparams("parallel", "parallel"),
        name="mla_attention",
    )(q, k, v, bias)


RET_UNROLL_MAX = 11


def _ret_kernel(q_ref, k_ref, v_ref, g_ref, dec_ref, o_ref, *, nblk):
    bf16 = jnp.bfloat16
    f32 = jnp.float32
    lane = lax.broadcasted_iota(jnp.int32, (BLOCK, LANES), 1)
    lane2 = lax.broadcasted_iota(jnp.int32, (BLOCK, 2 * RET_V), 1)
    feat = lax.broadcasted_iota(jnp.int32, (BLOCK, 2 * RET_V), 0)
    zero = jnp.zeros((BLOCK, LANES), bf16)
    zero2 = jnp.zeros((BLOCK, 2 * RET_V), bf16)
    d_in = jnp.concatenate([dec_ref[0, 0], dec_ref[1, 0]], axis=0)
    q_decay = jnp.concatenate([dec_ref[0, 1], dec_ref[1, 1]], axis=1)
    k_decay = jnp.where(lane < RET_QK, dec_ref[0, 2], dec_ref[1, 2])
    c_decay = jnp.concatenate([dec_ref[0, 3], dec_ref[1, 3]], axis=1)
    same_head = ((feat < RET_QK) == (lane2 < RET_V)).astype(f32)

    def chunks(ci, state):
        for u in range(unroll):
            state = chunk(ci * unroll + u, state)
        return state

    def chunk(c, state):
        rows = pl.ds(pl.multiple_of(c * BLOCK, BLOCK), BLOCK)
        qb = q_ref[0, rows, :]
        kb = k_ref[0, rows, :]
        vt = v_ref[0, rows, :]
        qstack = jnp.concatenate([jnp.where(lane < RET_QK, qb, zero), jnp.where(lane < RET_QK, zero, qb)], axis=0)
        inner = _dot_nt(qstack, kb) * d_in
        inner_cat = jnp.concatenate([inner[:BLOCK], inner[BLOCK:]], axis=1).astype(bf16)
        v_blocks = jnp.concatenate([jnp.where(lane2 < RET_V, vt, zero2), jnp.where(lane2 < RET_V, zero2, vt)], axis=0)
        y = _dot(inner_cat, v_blocks) + _dot(qb, state.astype(bf16)) * q_decay
        kd = (kb.astype(f32) * k_decay).astype(bf16)
        state = state * c_decay + _dot_tn(kd, vt) * same_head
        for hd in range(2):
            cols = slice(hd * RET_V, (hd + 1) * RET_V)
            yh = y[:, cols]
            mu = jnp.mean(yh, axis=-1, keepdims=True)
            yc = yh - mu
            var = jnp.mean(yc * yc, axis=-1, keepdims=True)
            yn = yc * lax.rsqrt(var + LN_EPS)
            gate = g_ref[0, rows, cols].astype(f32)
            o_ref[0, rows, cols] = (gate * jax.nn.sigmoid(gate) * yn).astype(o_ref.dtype)
        return state

    unroll = max(u for u in range(1, RET_UNROLL_MAX + 1) if nblk % u == 0)
    lax.fori_loop(0, nblk // unroll, chunks, jnp.zeros((BLOCK, 2 * RET_V), f32))


def _retention(q, k, v, g, dec):
    bsz, length, _ = v.shape
    qk = lambda: pl.BlockSpec((1, length, LANES), lambda b, p: (b, 0, p))
    vg = lambda: pl.BlockSpec((1, length, 2 * RET_V), lambda b, p: (b, 0, p))
    return pl.pallas_call(
        functools.partial(_ret_kernel, nblk=length // BLOCK),
        grid=(bsz, RET_HEADS // 2),
        in_specs=[qk(), qk(), vg(), vg(),
                  pl.BlockSpec((2, 4, BLOCK, LANES), lambda b, p: (p, 0, 0, 0))],
        out_specs=pl.BlockSpec((1, length, 2 * RET_V), lambda b, p: (b, 0, p)),
        out_shape=jax.ShapeDtypeStruct((bsz, length, D_RET), jnp.bfloat16),
        compiler_params=_cparams("parallel", "parallel"),
        name="retention",
    )(q, k, v, g, dec)


def _out_kernel(a_ref, b_ref, c_ref, h_ref, w_ref, g_ref, beta_ref, o_ref):
    bf16 = jnp.bfloat16
    mix = (_dot(a_ref[0].astype(bf16), w_ref[0:D_SB, :])
           + _dot(b_ref[0], w_ref[D_SB:D_SB + D_MLA, :])
           + _dot(c_ref[0], w_ref[D_SB + D_MLA:D_MIX, :]))
    o_ref[0] = _layer_norm_rows(DN_ALPHA * h_ref[0] + mix, g_ref[...], beta_ref[...])


def _out_projection(a, b, c, h, w, g, beta, layer):
    bsz, length, d = h.shape
    tm = _row_tile(length, 704)
    tok = lambda wd: pl.BlockSpec((1, tm, wd), lambda i, j: (i, j, 0))
    const2 = lambda i, j: (0, 0)
    return pl.pallas_call(
        _out_kernel,
        grid=(bsz, length // tm),
        in_specs=[tok(D_SB), tok(D_MLA), tok(D_RET), tok(d),
                  _layer_weight_spec(w, layer), pl.BlockSpec(g.shape, const2),
                  pl.BlockSpec(beta.shape, const2)],
        out_specs=tok(d),
        out_shape=jax.ShapeDtypeStruct(h.shape, jnp.float32),
        compiler_params=_cparams("parallel", "parallel"),
        name="out_projection",
    )(a, b, c, h, w, g, beta)


FF_CHUNK = 1024


def _mlp_kernel(h_ref, w1_ref, w2_ref, g_ref, beta_ref, o_ref):
    bf16 = jnp.bfloat16
    h = h_ref[0]
    hb = h.astype(bf16)
    acc = DN_ALPHA * h
    for lo in range(0, D_FF, FF_CHUNK):
        u = jnp.maximum(_dot(hb, w1_ref[:, lo:lo + FF_CHUNK]), 0.0)
        acc = acc + _dot((u * u).astype(bf16), w2_ref[lo:lo + FF_CHUNK, :])
    o_ref[0] = _layer_norm_rows(acc, g_ref[...], beta_ref[...])


def _mlp(h, w1, w2, g, beta, layer, skip_rows=0):
    bsz, length, d = h.shape
    rows_out = length - skip_rows
    tm = _row_tile(rows_out, 704)
    if skip_rows:
        h_spec = pl.BlockSpec((pl.Element(1), pl.Element(tm), pl.Element(d)),
                              lambda i, j: (i, pl.multiple_of(skip_rows + j * tm, BLOCK), 0))
    else:
        h_spec = pl.BlockSpec((1, tm, d), lambda i, j: (i, j, 0))
    const2 = lambda i, j: (0, 0)
    return pl.pallas_call(
        _mlp_kernel,
        grid=(bsz, rows_out // tm),
        in_specs=[h_spec, _layer_weight_spec(w1, layer), _layer_weight_spec(w2, layer),
                  pl.BlockSpec(g.shape, const2), pl.BlockSpec(beta.shape, const2)],
        out_specs=pl.BlockSpec((1, tm, d), lambda i, j: (i, j, 0)),
        out_shape=jax.ShapeDtypeStruct((bsz, rows_out, d), jnp.float32),
        compiler_params=_cparams("parallel", "parallel"),
        name="mlp",
    )(h, w1, w2, g, beta)


def _rope_tables(length):
    pos = (jnp.arange(length) - N_PAD).astype(jnp.float32)

    def angles(half):
        inv = ROPE_THETA ** (-jnp.arange(half, dtype=jnp.float32) / half)
        ang = pos[:, None] * inv[None, :]
        return jnp.cos(ang), jnp.sin(ang)

    cos, sin = angles(MLA_ROPE // 2)
    one = jnp.ones((length, MLA_NOPE), jnp.float32)
    z = lambda w: jnp.zeros((length, w), jnp.float32)
    mla = jnp.stack([jnp.concatenate([one, cos, cos, z(32)], 1),
                     jnp.concatenate([z(64), -sin, z(16), z(32)], 1),
                     jnp.concatenate([z(64), z(16), sin, z(32)], 1)])
    cos, sin = angles(RET_QK // 2)
    ret = jnp.stack([jnp.concatenate([cos, cos] * 2, 1),
                     jnp.concatenate([-sin, z(32)] * 2, 1),
                     jnp.concatenate([z(32), sin] * 2, 1)])
    return mla, ret


def _retention_decays():
    log_g = jnp.log(jnp.array(RET_GAMMA, jnp.float32))
    idx = jnp.arange(BLOCK, dtype=jnp.float32)
    diff = idx[:, None] - idx[None, :]
    d_in = jnp.where(diff[None] >= 0, jnp.exp(jnp.maximum(diff, 0.0)[None] * log_g[:, None, None]), 0.0)
    q_decay = jnp.exp((idx[None, :] + 1.0) * log_g[:, None])
    k_decay = jnp.exp((BLOCK - 1.0 - idx[None, :]) * log_g[:, None])
    c_decay = jnp.exp(BLOCK * log_g)
    rep = lambda a: jnp.broadcast_to(a[:, :, None], (RET_HEADS, BLOCK, LANES))
    full = jnp.broadcast_to(c_decay[:, None, None], (RET_HEADS, BLOCK, LANES))
    return jnp.stack([d_in, rep(q_decay), rep(k_decay), full], axis=1)


def _suffix_sum_matrix():
    i = np.arange(BLOCK)
    upper = (i[:, None] > i[None, :]).astype(np.float32)
    ones = np.ones((BLOCK, BLOCK), np.float32)
    half = np.concatenate([upper, ones], axis=1)
    return jnp.asarray(np.concatenate([half, half], axis=0), jnp.bfloat16)


def kernel(x, meta_tokens, ln_emb_g, ln_emb_b, w_in, mla_q_norm, mla_kv_norm, w_uq, w_ukv, w_out,
           ln1_g, ln1_b, w_ff1, w_ff2, ln2_g, ln2_b):
    bsz, seq, d = x.shape
    assert d == D_MODEL
    bf16 = jnp.bfloat16
    depth = w_in.shape[0]

    length = N_PAD + N_META + seq
    assert length % BLOCK == 0

    zc = lambda n: jnp.zeros((depth, d, n), w_in.dtype)
    k_r_lo = 3 * D_SB + MLA_Q_LORA + MLA_KV_LORA
    win = jnp.concatenate([w_in[..., :k_r_lo], zc(MLA_NOPE), w_in[..., k_r_lo:k_r_lo + MLA_ROPE],
                           zc(LANES - MLA_NOPE - MLA_ROPE), w_in[..., k_r_lo + MLA_ROPE:]], axis=-1).astype(bf16)
    assert win.shape[-1] == N_IN_PAD
    wuq = w_uq.reshape(depth, MLA_Q_LORA, MLA_HEADS, MLA_NOPE + MLA_ROPE)
    wuq = jnp.pad(wuq, ((0, 0), (0, 0), (0, 0), (0, MLA_QK_PAD - MLA_NOPE - MLA_ROPE)))
    wuq = wuq.reshape(depth, MLA_Q_LORA, MLA_HEADS * MLA_QK_PAD).astype(bf16)
    wukv = w_ukv.reshape(depth, MLA_KV_LORA, MLA_HEADS, MLA_NOPE + MLA_V)
    wuk = jnp.pad(wukv[..., :MLA_NOPE], ((0, 0), (0, 0), (0, 0), (0, MLA_QK_PAD - MLA_NOPE)))
    wuk = wuk.reshape(depth, MLA_KV_LORA, MLA_HEADS * MLA_QK_PAD).astype(bf16)
    wuv = wukv[..., MLA_NOPE:].reshape(depth, MLA_KV_LORA, D_MLA).astype(bf16)
    wout = w_out.astype(bf16)
    w1 = w_ff1.astype(bf16)
    w2 = w_ff2.astype(bf16)

    mtab, rtab = _rope_tables(length)
    dec = _retention_decays()
    uo = _suffix_sum_matrix()
    row2 = lambda a: a.reshape(1, -1)

    h = _embed_layer_norm(x, meta_tokens, ln_emb_g, ln_emb_b)
    for l in range(depth):
        sbq, sbk, sbv, mq, mk, mv, rq, rk, rv, rg = _projections(
            h, win, row2(mla_q_norm[l]), row2(mla_kv_norm[l]), wuq, wuk, wuv, mtab, rtab, l)
        out_a = _stick_breaking(sbq, sbk, sbv, uo)
        out_b = _mla_attention(mq, mk, mv)
        out_c = _retention(rq, rk, rv, rg, dec)
        h = _out_projection(out_a, out_b, out_c, h, wout, row2(ln1_g[l]), row2(ln1_b[l]), l)
        h = _mlp(h, w1, w2, row2(ln2_g[l]), row2(ln2_b[l]), l, skip_rows=N_PAD + N_META if l == depth - 1 else 0)
    return h
```
